```python
import math
import jax, jax.numpy as jnp
from jax import lax
import numpy as np

D_MODEL = 1024
BATCH = 1
SEQ = 16384
DEPTH = 2
DEC_BATCH = 32
DEC_SEQ = 32
PAST_LEN = 2048

CHUNK = 64
SSD_EXPAND = 2
D_SSM = SSD_EXPAND * D_MODEL
SSD_HEAD_DIM = 64
SSD_HEADS = D_SSM // SSD_HEAD_DIM
SSD_GROUPS = 4
HEADS_PER_GROUP = SSD_HEADS // SSD_GROUPS
SSD_STATE = 128
SSD_CONV = 4
SSD_CONV_DIM = D_SSM + 2 * SSD_GROUPS * SSD_STATE
SB_HEAD_DIM = 128
SB_HEADS = D_MODEL // SB_HEAD_DIM
SB_WIDTH = SB_HEADS * SB_HEAD_DIM
SB_BLOCK = 128
D_FF = 4 * D_MODEL
FFN_CONV = 3
PLE_DIM = 256
IN_DIM = D_SSM + SSD_CONV_DIM + SSD_HEADS + 3 * SB_WIDTH + 2 * D_MODEL
EPS = 1e-6

kernel_name = 'hybrid_ssd_stickbreak_convffn_stream_step'


def rmsnorm(x, gain):
    xf = x.astype(jnp.float32)
    xf = xf * lax.rsqrt(jnp.mean(xf * xf, axis=-1, keepdims=True) + EPS)
    return (xf * gain.astype(jnp.float32)).astype(x.dtype)


def causal_dwconv(xp, w, b):
    width = w.shape[0]
    length = xp.shape[1] - (width - 1)
    out = b
    for tap in range(width):
        out = out + xp[:, tap:tap + length] * w[tap]
    return out


def gated_group_rmsnorm(y, z, w):
    g = (y * jax.nn.silu(z)).astype(jnp.float32)
    shp = g.shape
    g = g.reshape(shp[:-1] + (SSD_GROUPS, D_SSM // SSD_GROUPS))
    g = g * lax.rsqrt(jnp.mean(g * g, axis=-1, keepdims=True) + EPS)
    return (g.reshape(shp) * w.astype(jnp.float32)).astype(y.dtype)


def ssd_chunked_scan(xs, dt, a, bm, cm, h0):
    f32 = jnp.float32
    bsz, seqlen = xs.shape[:2]
    cl = CHUNK if seqlen >= CHUNK else seqlen
    nc = seqlen // cl
    xg = xs.astype(f32).reshape(bsz, nc, cl, SSD_GROUPS, HEADS_PER_GROUP, SSD_HEAD_DIM)
    dtg = dt.astype(f32).reshape(bsz, nc, cl, SSD_GROUPS, HEADS_PER_GROUP)
    ag = dtg * a.astype(f32).reshape(SSD_GROUPS, HEADS_PER_GROUP)
    bg = bm.astype(f32).reshape(bsz, nc, cl, SSD_GROUPS, SSD_STATE)
    cg = cm.astype(f32).reshape(bsz, nc, cl, SSD_GROUPS, SSD_STATE)
    a_cum = jnp.cumsum(ag, axis=2)
    x_dt = xg * dtg[..., None]
    causal = jnp.tril(jnp.ones((cl, cl), dtype=bool))[None, None, :, :, None, None]
    seg = a_cum[:, :, :, None] - a_cum[:, :, None, :]
    decay_ls = jnp.exp(jnp.where(causal, seg, -jnp.inf))
    cb = jnp.einsum('bclgn,bcsgn->bclsg', cg, bg)
    y_diag = jnp.einsum('bclsge,bcsgep->bclgep', cb[..., None] * decay_ls, x_dt)
    decay_to_end = jnp.exp(a_cum[:, :, -1:] - a_cum)
    chunk_states = jnp.einsum('bclgn,bclgep->bcgepn', bg, x_dt * decay_to_end[..., None])
    chunk_decay = jnp.exp(a_cum[:, :, -1])

    def step(h, inp):
        dec, st = inp
        return dec[..., None, None] * h + st, h

    h_init = h0.astype(f32).reshape(bsz, SSD_GROUPS, HEADS_PER_GROUP, SSD_HEAD_DIM, SSD_STATE)
    h_last, h_prev = lax.scan(step, h_init,
                              (jnp.moveaxis(chunk_decay, 1, 0), jnp.moveaxis(chunk_states, 1, 0)))
    h_prev = jnp.moveaxis(h_prev, 0, 1)
    y_off = jnp.einsum('bclgn,bcgepn->bclgep', cg, h_prev) * jnp.exp(a_cum)[..., None]
    y = (y_diag + y_off).reshape(bsz, seqlen, SSD_HEADS, SSD_HEAD_DIM)
    return y, h_last.reshape(bsz, SSD_HEADS, SSD_HEAD_DIM, SSD_STATE)


def stick_breaking(q, k, v, q_start):
    f32 = jnp.float32
    bsz, lq = q.shape[:2]
    lk = k.shape[1]
    nkb_total = -(-lk // SB_BLOCK)
    pad = nkb_total * SB_BLOCK - lk
    k = jnp.pad(k, ((0, 0), (0, pad), (0, 0), (0, 0)))
    v = jnp.pad(v, ((0, 0), (0, pad), (0, 0), (0, 0)))
    blk = SB_BLOCK if lq % SB_BLOCK == 0 else lq
    nb = lq // blk
    scale = SB_HEAD_DIM ** -0.5
    idx = jnp.arange(SB_BLOCK)
    tri_in = (idx[:, None] > idx[None, :]).astype(f32)
    outs = []
    for bi in range(nb):
        q_lo = q_start + bi * blk
        nkb = max(1, min(nkb_total, -(-(q_lo + blk - 1) // SB_BLOCK)))
        kk = k[:, :nkb * SB_BLOCK]
        vv = v[:, :nkb * SB_BLOCK]
        q_blk = q[:, bi * blk:(bi + 1) * blk]
        z = jnp.einsum('bqhd,bkhd->bhqk', q_blk, kk).astype(f32) * scale
        z = z.reshape(bsz, SB_HEADS, blk, nkb, SB_BLOCK)
        q_pos = q_lo + jnp.arange(blk)
        k_pos = jnp.arange(nkb * SB_BLOCK).reshape(nkb, SB_BLOCK)
        earlier = k_pos[None, None, None] < q_pos[None, None, :, None, None]
        log_keep = jnp.where(earlier, jax.nn.log_sigmoid(-z), 0.0)
        within = jnp.einsum('bhqcj,js->bhqcs', log_keep, tri_in)
        totals = jnp.sum(log_keep, axis=-1)
        cidx = jnp.arange(nkb)
        tri_blk = (cidx[:, None] > cidx[None, :]).astype(f32)
        later = jnp.einsum('bhqd,dc->bhqc', totals, tri_blk)
        weights = jnp.where(earlier,
                            jnp.exp(jax.nn.log_sigmoid(z) + within + later[..., None]), 0.0)
        weights = weights.reshape(bsz, SB_HEADS, blk, nkb * SB_BLOCK)
        outs.append(jnp.einsum('bhqk,bkhd->bqhd', weights.astype(v.dtype), vv))
    o = jnp.concatenate(outs, axis=1) if nb > 1 else outs[0]
    return o.reshape(bsz, lq, SB_WIDTH)


def trunk_layer(x, p_i, conv_st, ssm_st, k_past, v_past, ffn_st, lw):
    bsz, length, _ = x.shape
    h = rmsnorm(x, lw['norm_pre_mix'])
    proj = h @ lw['w_in']
    widths = (D_SSM, SSD_CONV_DIM, SSD_HEADS, SB_WIDTH, SB_WIDTH, SB_WIDTH, 2 * D_MODEL)
    offsets = [int(o) for o in np.cumsum(widths)[:-1]]
    z, xbc, dt_raw, q, k, v, gate_logits = jnp.split(proj, offsets, axis=-1)
    xbc_pad = jnp.concatenate([conv_st, xbc], axis=1)
    new_conv = xbc_pad[:, -(SSD_CONV - 1):]
    xbc_c = jax.nn.silu(causal_dwconv(xbc_pad, lw['ssd_conv_w'], lw['ssd_conv_b']))
    xs, bm, cm = jnp.split(xbc_c, [D_SSM, D_SSM + SSD_GROUPS * SSD_STATE], axis=-1)
    dt = jax.nn.softplus(dt_raw.astype(jnp.float32) + lw['ssd_dt_bias'].astype(jnp.float32))
    a = -jnp.exp(lw['ssd_a_log'].astype(jnp.float32))
    xs_h = xs.reshape(bsz, length, SSD_HEADS, SSD_HEAD_DIM)
    y, new_ssm = ssd_chunked_scan(xs_h, dt, a,
                                  bm.reshape(bsz, length, SSD_GROUPS, SSD_STATE),
                                  cm.reshape(bsz, length, SSD_GROUPS, SSD_STATE), ssm_st)
    y = (y + lw['ssd_d'].astype(jnp.float32)[:, None] * xs_h.astype(jnp.float32)).astype(x.dtype)
    y = gated_group_rmsnorm(y.reshape(bsz, length, D_SSM), z, lw['ssd_norm'])
    branch_ssd = y @ lw['w_br_ssd']
    q = q.reshape(bsz, length, SB_HEADS, SB_HEAD_DIM)
    k_new = k.reshape(bsz, length, SB_HEADS, SB_HEAD_DIM)
    v_new = v.reshape(bsz, length, SB_HEADS, SB_HEAD_DIM)
    k_all = jnp.concatenate([k_past, k_new], axis=1)
    v_all = jnp.concatenate([v_past, v_new], axis=1)
    branch_sb = stick_breaking(q, k_all, v_all, k_past.shape[1]) @ lw['w_br_sb']
    g_ssd, g_sb = jnp.split(jax.nn.sigmoid(gate_logits), 2, axis=-1)
    mixed = (g_ssd * branch_ssd + g_sb * branch_sb) @ lw['w_out']
    x = x + rmsnorm(mixed, lw['norm_post_mix'])
    h2 = rmsnorm(x, lw['norm_pre_ffn'])
    ff_gate, ff_up = jnp.split(h2 @ lw['w_up'], 2, axis=-1)
    gate_pad = jnp.concatenate([ffn_st, ff_gate], axis=1)
    new_ffn = gate_pad[:, -(FFN_CONV - 1):]
    ff = jax.nn.gelu(causal_dwconv(gate_pad, lw['ffn_conv_w'], lw['ffn_conv_b']), approximate=True) * ff_up
    x = x + rmsnorm(ff @ lw['w_down'], lw['norm_post_ffn'])
    ple = (p_i @ lw['w_ple']) * jax.nn.sigmoid(x @ lw['w_ple_gate'])
    x = x + rmsnorm(ple, lw['norm_ple'])
    return x, (new_conv, new_ssm.astype(ssm_st.dtype), k_new, v_new, new_ffn)


def setup_inputs(seed: int = 0) -> dict:
    key = jax.random.key(seed)
    ks = iter(jax.random.split(key, 40))

    def nrm(shape, scale=1.0):
        return jax.random.normal(next(ks), shape, jnp.float32) * scale

    def gain(shape):
        return 1.0 + nrm(shape, 0.05)

    dt_u = jax.random.uniform(next(ks), (DEPTH, SSD_HEADS), jnp.float32)
    dt0 = jnp.exp(dt_u * (math.log(0.1) - math.log(0.001)) + math.log(0.001))
    dt_bias = dt0 + jnp.log(-jnp.expm1(-dt0))
    a_log = jnp.log(jax.random.uniform(next(ks), (DEPTH, SSD_HEADS), jnp.float32, minval=1.0, maxval=16.0))
    return {
        'x_prompt': nrm((BATCH, SEQ, D_MODEL)),
        'x_sample': nrm((DEC_BATCH, DEC_SEQ, D_MODEL)),
        'state_ssd_conv': nrm((DEPTH, DEC_BATCH, SSD_CONV - 1, SSD_CONV_DIM)),
        'state_ssd': nrm((DEPTH, DEC_BATCH, SSD_HEADS, SSD_HEAD_DIM, SSD_STATE), 0.1),
        'cache_sb_k': nrm((DEPTH, DEC_BATCH, PAST_LEN, SB_HEADS, SB_HEAD_DIM)),
        'cache_sb_v': nrm((DEPTH, DEC_BATCH, PAST_LEN, SB_HEADS, SB_HEAD_DIM)),
        'state_ffn_conv': nrm((DEPTH, DEC_BATCH, FFN_CONV - 1, D_FF)),
        'p_prompt': nrm((DEPTH, BATCH, SEQ, PLE_DIM)),
        'p_sample': nrm((DEPTH, DEC_BATCH, DEC_SEQ, PLE_DIM)),
        'norm_pre_mix': gain((DEPTH, D_MODEL)),
        'w_in': nrm((DEPTH, D_MODEL, IN_DIM), D_MODEL ** -0.5),
        'ssd_conv_w': nrm((DEPTH, SSD_CONV, SSD_CONV_DIM), SSD_CONV ** -0.5),
        'ssd_conv_b': nrm((DEPTH, SSD_CONV_DIM), 0.02),
        'ssd_dt_bias': dt_bias,
        'ssd_a_log': a_log,
        'ssd_d': gain((DEPTH, SSD_HEADS)),
        'ssd_norm': gain((DEPTH, D_SSM)),
        'w_br_ssd': nrm((DEPTH, D_SSM, D_MODEL), D_SSM ** -0.5),
        'w_br_sb': nrm((DEPTH, SB_WIDTH, D_MODEL), SB_WIDTH ** -0.5),
        'w_out': nrm((DEPTH, D_MODEL, D_MODEL), D_MODEL ** -0.5),
        'norm_post_mix': gain((DEPTH, D_MODEL)),
        'norm_pre_ffn': gain((DEPTH, D_MODEL)),
        'w_up': nrm((DEPTH, D_MODEL, 2 * D_FF), D_MODEL ** -0.5),
        'ffn_conv_w': nrm((DEPTH, FFN_CONV, D_FF), FFN_CONV ** -0.5),
        'ffn_conv_b': nrm((DEPTH, D_FF), 0.02),
        'w_down': nrm((DEPTH, D_FF, D_MODEL), D_FF ** -0.5),
        'norm_post_ffn': gain((DEPTH, D_MODEL)),
        'w_ple': nrm((DEPTH, PLE_DIM, D_MODEL), PLE_DIM ** -0.5),
        'w_ple_gate': nrm((DEPTH, D_MODEL, D_MODEL), D_MODEL ** -0.5),
        'norm_ple': gain((DEPTH, D_MODEL)),
    }


def reference(x_prompt, x_sample, state_ssd_conv, state_ssd, cache_sb_k, cache_sb_v, state_ffn_conv,
              p_prompt, p_sample, norm_pre_mix, w_in, ssd_conv_w, ssd_conv_b, ssd_dt_bias, ssd_a_log,
              ssd_d, ssd_norm, w_br_ssd, w_br_sb, w_out, norm_post_mix, norm_pre_ffn, w_up, ffn_conv_w,
              ffn_conv_b, w_down, norm_post_ffn, w_ple, w_ple_gate, norm_ple):
    def run(x, p, conv0, ssm0, k0, v0, ffn0):
        per_layer = []
        for i in range(DEPTH):
            lw = {
                'norm_pre_mix': norm_pre_mix[i], 'w_in': w_in[i],
                'ssd_conv_w': ssd_conv_w[i], 'ssd_conv_b': ssd_conv_b[i],
                'ssd_dt_bias': ssd_dt_bias[i], 'ssd_a_log': ssd_a_log[i], 'ssd_d': ssd_d[i],
                'ssd_norm': ssd_norm[i], 'w_br_ssd': w_br_ssd[i], 'w_br_sb': w_br_sb[i],
                'w_out': w_out[i], 'norm_post_mix': norm_post_mix[i],
                'norm_pre_ffn': norm_pre_ffn[i], 'w_up': w_up[i],
                'ffn_conv_w': ffn_conv_w[i], 'ffn_conv_b': ffn_conv_b[i],
                'w_down': w_down[i], 'norm_post_ffn': norm_post_ffn[i],
                'w_ple': w_ple[i], 'w_ple_gate': w_ple_gate[i], 'norm_ple': norm_ple[i],
            }
            x, st = trunk_layer(x, p[i], conv0[i], ssm0[i], k0[i], v0[i], ffn0[i], lw)
            per_layer.append(st)
        stacked = [jnp.stack([st[j] for st in per_layer]) for j in range(5)]
        return x, stacked

    bp = x_prompt.shape[0]
    dtp = x_prompt.dtype
    zero_conv = jnp.zeros((DEPTH, bp, SSD_CONV - 1, SSD_CONV_DIM), dtp)
    zero_ssm = jnp.zeros((DEPTH, bp, SSD_HEADS, SSD_HEAD_DIM, SSD_STATE), dtp)
    empty_kv = jnp.zeros((DEPTH, bp, 0, SB_HEADS, SB_HEAD_DIM), dtp)
    zero_ffn = jnp.zeros((DEPTH, bp, FFN_CONV - 1, D_FF), dtp)
    y_prompt, prompt_states = run(x_prompt, p_prompt, zero_conv, zero_ssm, empty_kv, empty_kv, zero_ffn)
    y_sample, sample_states = run(x_sample, p_sample, state_ssd_conv, state_ssd, cache_sb_k, cache_sb_v,
                                  state_ffn_conv)
    prompt_ssd_conv, prompt_ssd_state, prompt_sb_k, prompt_sb_v, prompt_ffn_conv = prompt_states
    sample_ssd_conv, sample_ssd_state, sample_sb_k, sample_sb_v, sample_ffn_conv = sample_states
    return (y_prompt, y_sample,
            prompt_ssd_conv, prompt_ssd_state, prompt_sb_k, prompt_sb_v, prompt_ffn_conv,
            sample_ssd_conv, sample_ssd_state, sample_sb_k, sample_sb_v, sample_ffn_conv)
```

```python
import functools

import jax
import jax.numpy as jnp
from jax import lax
from jax.experimental import pallas as pl
from jax.experimental.pallas import tpu as pltpu

F32 = jnp.float32
BF16 = jnp.bfloat16

D_MODEL = 1024
DEPTH = 2
CHUNK = 64
D_SSM = 2048
SSD_HEAD_DIM = 64
SSD_HEADS = 32
SSD_GROUPS = 4
HEADS_PER_GROUP = 8
SSD_STATE = 128
SSD_CONV = 4
SSD_CONV_DIM = D_SSM + 2 * SSD_GROUPS * SSD_STATE
GROUP_WIDTH = D_SSM // SSD_GROUPS
SB_HEAD_DIM = 128
SB_HEADS = 8
SB_WIDTH = 1024
SB_BLOCK = 128
D_FF = 4096
FFN_CONV = 3
PLE_DIM = 256
EPS = 1e-6
SB_SCALE = SB_HEAD_DIM ** -0.5
SB_DEAD_LOG = 120.0

LANES = 128
SUBLANES = 8
VMEM_LIMIT = 56 * 1024 * 1024
FFN_ROWS = 1024


def _cparams(sem):
    return pltpu.CompilerParams(dimension_semantics=sem, vmem_limit_bytes=VMEM_LIMIT)


def _rms(x, gain):
    return x * lax.rsqrt(jnp.mean(x * x, axis=-1, keepdims=True) + EPS) * gain


def _sigmoid(x):
    return 1.0 / (1.0 + jnp.exp(-x))


def _dot(a, b):
    return jnp.dot(a, b, preferred_element_type=F32)


def _dot_nt(a, b):
    return lax.dot_general(a, b, (((1,), (1,)), ((), ())), preferred_element_type=F32)


def _dot_tn(a, b):
    return lax.dot_general(a, b, (((0,), (0,)), ((), ())), preferred_element_type=F32)


def _split_bf16(v, n):
    parts = []
    r = v
    for _ in range(n):
        p = r.astype(BF16)
        parts.append(p)
        r = r - p.astype(F32)
    return parts


def _shifted_rows(x, hist, k):
    nseq = hist.shape[0]
    tl = x.shape[0] // nseq
    rolled = pltpu.roll(x, k, 0)
    row = lax.broadcasted_iota(jnp.int32, (SUBLANES, x.shape[1]), 0)
    pieces = []
    for b in range(nseq):
        start = b * tl
        pieces.append(jnp.where(row < k, pltpu.roll(hist[b], k, 0), rolled[start:start + SUBLANES]))
        if tl > SUBLANES:
            pieces.append(rolled[start + SUBLANES:start + tl])
    return pieces[0] if len(pieces) == 1 else jnp.concatenate(pieces, axis=0)


def _causal_conv(x, hist, w_ref, b_ref, width):
    acc = b_ref[...]
    for tap in range(width):
        k = width - 1 - tap
        sh = x if k == 0 else _shifted_rows(x, hist, k)
        acc = acc + sh * w_ref[tap:tap + 1, :]
    return acc


def _norm_matmul_kernel(x_ref, g_ref, w_ref, o_ref, h_ref, *, normalize):
    @pl.when(pl.program_id(1) == 0)
    def _():
        x = x_ref[...]
        if normalize:
            x = _rms(x, g_ref[...])
        h_ref[...] = x.astype(BF16)

    o_ref[...] = _dot(h_ref[...], w_ref[...])


def _norm_matmul(x, gain, w, *, normalize=True):
    m, k = x.shape
    n = w.shape[1]
    tm = min(1024, m)
    tn = min(1024, n)
    return pl.pallas_call(
        functools.partial(_norm_matmul_kernel, normalize=normalize),
        out_shape=jax.ShapeDtypeStruct((m, n), F32),
        grid=(m // tm, n // tn),
        in_specs=[pl.BlockSpec((tm, k), lambda i, j: (i, 0)),
                  pl.BlockSpec((1, k), lambda i, j: (0, 0)),
                  pl.BlockSpec((k, tn), lambda i, j: (0, j))],
        out_specs=pl.BlockSpec((tm, tn), lambda i, j: (i, j)),
        scratch_shapes=[pltpu.VMEM((tm, k), BF16)],
        compiler_params=_cparams(("parallel", "arbitrary")),
        name="norm_matmul",
    )(x, gain, w)


def _ssd_kernel(xbc_ref, dt_ref, cst_ref, cw_ref, cb_ref, dtb_ref, alog_ref, dexp_ref, h0_ref, e_ref, tri_ref,
                y_ref, hout_ref, h_scr, carry_scr, xc_scr, dt_scr, *, cl, nchunk):
    l = pl.program_id(1)

    @pl.when(l == 0)
    def _():
        h_scr[...] = h0_ref[0]
        carry_scr[...] = cst_ref[0]

    xt = xbc_ref[0]
    tl = xt.shape[0]
    conv = _causal_conv(xt, carry_scr[...][None], cw_ref, cb_ref, SSD_CONV)
    carry_scr[...] = xt[tl - SUBLANES:tl]
    xc_scr[...] = conv * _sigmoid(conv)
    dtr = dt_ref[0] + dtb_ref[...]
    dt_scr[...] = jnp.maximum(dtr, 0.0) + jnp.log1p(jnp.exp(-jnp.abs(dtr)))

    a_head = -jnp.exp(alog_ref[...])
    tri = tri_ref[...]
    expand = e_ref[...]
    rowi = lax.broadcasted_iota(jnp.int32, (cl, cl), 0)
    coli = lax.broadcasted_iota(jnp.int32, (cl, cl), 1)
    causal = coli <= rowi
    lane = lax.broadcasted_iota(jnp.int32, (1, LANES), 1)
    half_masks = ((lane < SSD_HEAD_DIM).astype(F32), (lane >= SSD_HEAD_DIM).astype(F32))

    def expand_heads(v):
        hi, lo = _split_bf16(v, 2)
        return _dot(hi, expand) + _dot(lo, expand)

    def chunk(r0):
        rows = pl.ds(r0, cl)
        xs = xc_scr[rows, 0:D_SSM]
        bm = xc_scr[rows, D_SSM:D_SSM + SSD_GROUPS * SSD_STATE]
        cm = xc_scr[rows, D_SSM + SSD_GROUPS * SSD_STATE:SSD_CONV_DIM]
        dtc = dt_scr[rows, :]
        a = dtc * a_head
        acum = sum(_dot(tri, p) for p in _split_bf16(a, 3))
        if cl < LANES:
            acum_sq = jnp.concatenate([acum, jnp.zeros((LANES - cl, LANES), F32)], axis=0)
        else:
            acum_sq = acum
        acum_t = acum_sq.T
        a_end = acum[cl - 1:cl, :]
        dt_e = expand_heads(dtc)
        dec_end_e = expand_heads(jnp.exp(a_end - acum))
        dec_in_e = expand_heads(jnp.exp(acum))
        chunk_decay = jnp.broadcast_to(jnp.exp(acum_t[:, cl - 1:cl]), (LANES, LANES))
        x_dt = xs * dt_e
        x_end = x_dt * dec_end_e
        for g in range(SSD_GROUPS):
            gs = slice(g * SSD_STATE, (g + 1) * SSD_STATE)
            gw = slice(g * GROUP_WIDTH, (g + 1) * GROUP_WIDTH)
            bg = bm[:, gs].astype(BF16)
            cg = cm[:, gs].astype(BF16)
            cb = _dot_nt(cg, bg)
            h_prev = h_scr[gw, :]
            y_off = _dot_nt(cg, h_prev.astype(BF16)) * dec_in_e[:, gw]
            st = _dot_tn(x_end[:, gw].astype(BF16), bg)
            for pr in range(HEADS_PER_GROUP // 2):
                ps = slice(g * GROUP_WIDTH + pr * LANES, g * GROUP_WIDTH + (pr + 1) * LANES)
                xp = x_dt[:, ps]
                yp = y_off[:, pr * LANES:(pr + 1) * LANES]
                for sub in range(2):
                    e = g * HEADS_PER_GROUP + pr * 2 + sub
                    seg = acum[:, e:e + 1] - acum_t[e:e + 1, 0:cl]
                    dec = jnp.exp(jnp.where(causal, seg, -jnp.inf))
                    m_e = (cb * dec).astype(BF16)
                    yp = yp + _dot(m_e, (xp * half_masks[sub]).astype(BF16))
                    hs = slice(e * SSD_HEAD_DIM, (e + 1) * SSD_HEAD_DIM)
                    ls = slice((pr * 2 + sub) * SSD_HEAD_DIM, (pr * 2 + sub + 1) * SSD_HEAD_DIM)
                    h_scr[hs, :] = h_scr[hs, :] * chunk_decay[e:e + 1, :] + st[ls, :]
                y_ref[0, rows, ps] = yp + dexp_ref[:, ps] * xs[:, ps]

    if nchunk == 1:
        chunk(0)
    else:
        def body(c, carry):
            chunk(pl.multiple_of(c * cl, cl))
            return carry
        lax.fori_loop(0, nchunk, body, 0)

    @pl.when(l == pl.num_programs(1) - 1)
    def _():
        hout_ref[0] = h_scr[...]


def _ssd(xbc, dt_raw, conv_state8, conv_w, conv_b, dt_bias, a_log, d_exp, h0, expand, tri, *, tl, cl):
    bsz, length, _ = xbc.shape
    kernel = functools.partial(_ssd_kernel, cl=cl, nchunk=tl // cl)
    full2 = lambda b, l: (0, 0)
    return pl.pallas_call(
        kernel,
        out_shape=(jax.ShapeDtypeStruct((bsz, length, D_SSM), F32),
                   jax.ShapeDtypeStruct((bsz, D_SSM, SSD_STATE), F32)),
        grid=(bsz, length // tl),
        in_specs=[pl.BlockSpec((1, tl, SSD_CONV_DIM), lambda b, l: (b, l, 0)),
                  pl.BlockSpec((1, tl, LANES), lambda b, l: (b, l, 0)),
                  pl.BlockSpec((1, SUBLANES, SSD_CONV_DIM), lambda b, l: (b, 0, 0)),
                  pl.BlockSpec((SSD_CONV, SSD_CONV_DIM), full2),
                  pl.BlockSpec((1, SSD_CONV_DIM), full2),
                  pl.BlockSpec((1, LANES), full2),
                  pl.BlockSpec((1, LANES), full2),
                  pl.BlockSpec((1, D_SSM), full2),
                  pl.BlockSpec((1, D_SSM, SSD_STATE), lambda b, l: (b, 0, 0)),
                  pl.BlockSpec((LANES, D_SSM), full2),
                  pl.BlockSpec((cl, cl), full2)],
        out_specs=(pl.BlockSpec((1, tl, D_SSM), lambda b, l: (b, l, 0)),
                   pl.BlockSpec((1, D_SSM, SSD_STATE), lambda b, l: (b, 0, 0))),
        scratch_shapes=[pltpu.VMEM((D_SSM, SSD_STATE), F32),
                        pltpu.VMEM((SUBLANES, SSD_CONV_DIM), F32),
                        pltpu.VMEM((tl, SSD_CONV_DIM), F32),
                        pltpu.VMEM((tl, LANES), F32)],
        compiler_params=_cparams(("parallel", "arbitrary")),
        name="ssd",
    )(xbc, dt_raw, conv_state8, conv_w, conv_b, dt_bias, a_log, d_exp, h0, expand, tri)


def _sb_block(q, kb, vb, later, acc, tri_ext, earlier):
    z = _dot_nt(q, kb) * SB_SCALE
    l1p = jnp.log(1.0 + jnp.exp(-jnp.abs(z)))
    log_keep = -(jnp.maximum(z, 0.0) + l1p)
    log_beta = jnp.minimum(z, 0.0) - l1p
    if earlier is not None:
        log_keep = jnp.where(earlier, log_keep, 0.0)
    ext = _dot(log_keep.astype(BF16), tri_ext)
    w = jnp.exp(log_beta + ext[:, :SB_BLOCK] + later)
    if earlier is not None:
        w = jnp.where(earlier, w, 0.0)
    acc = acc + _dot(w.astype(BF16), vb)
    return later + ext[:, SB_BLOCK:], acc


def _sb_more(carry, *, n):
    i, later, _ = carry
    return jnp.logical_and(i < n, jnp.max(later) > -SB_DEAD_LOG)


def _sb_prompt_kernel(q_ref, k_ref, v_ref, tri_ref, o_ref, *, tq):
    qi = pl.program_id(2)
    q = q_ref[0].astype(BF16)
    tri_ext = tri_ref[...]
    nsub = tq // SB_BLOCK
    dmr = (lax.broadcasted_iota(jnp.int32, (tq, SB_BLOCK), 1)
           - lax.broadcasted_iota(jnp.int32, (tq, SB_BLOCK), 0))
    later = jnp.zeros((tq, SB_BLOCK), F32)
    acc = jnp.zeros((tq, SB_HEAD_DIM), F32)
    q0 = qi * tq
    for s in reversed(range(nsub)):
        rows = pl.ds(pl.multiple_of(q0 + s * SB_BLOCK, SB_BLOCK), SB_BLOCK)
        later, acc = _sb_block(q, k_ref[0, rows, :], v_ref[0, rows, :], later, acc, tri_ext,
                               dmr < -(s * SB_BLOCK))

    def body(carry):
        i, later, acc = carry
        base = (qi - 1 - i) * tq
        for s in reversed(range(nsub)):
            rows = pl.ds(pl.multiple_of(base + s * SB_BLOCK, SB_BLOCK), SB_BLOCK)
            later, acc = _sb_block(q, k_ref[0, rows, :], v_ref[0, rows, :], later, acc, tri_ext, None)
        return i + 1, later, acc

    _, later, acc = lax.while_loop(functools.partial(_sb_more, n=qi), body, (0, later, acc))
    o_ref[0] = acc


def _sb_prompt(qkv, kv_bf16, tri_ext, *, tq):
    bsz, length, _ = qkv.shape
    return pl.pallas_call(
        functools.partial(_sb_prompt_kernel, tq=tq),
        out_shape=jax.ShapeDtypeStruct((bsz, length, SB_WIDTH), F32),
        grid=(bsz, SB_HEADS, length // tq),
        in_specs=[pl.BlockSpec((1, tq, SB_HEAD_DIM), lambda b, h, i: (b, i, h)),
                  pl.BlockSpec((1, length, SB_HEAD_DIM), lambda b, h, i: (b, 0, h)),
                  pl.BlockSpec((1, length, SB_HEAD_DIM), lambda b, h, i: (b, 0, SB_HEADS + h)),
                  pl.BlockSpec((SB_BLOCK, 2 * SB_BLOCK), lambda b, h, i: (0, 0))],
        out_specs=pl.BlockSpec((1, tq, SB_HEAD_DIM), lambda b, h, i: (b, i, h)),
        compiler_params=_cparams(("parallel", "parallel", "arbitrary")),
        name="sb_prompt",
    )(qkv, kv_bf16, kv_bf16, tri_ext)


def _sb_sample_kernel(q_ref, kn_ref, vn_ref, kp_ref, vp_ref, tri_ref, o_ref, *, npast):
    q = q_ref[0].astype(BF16)
    tq = q.shape[0]
    tri_ext = tri_ref[...]
    dmr = (lax.broadcasted_iota(jnp.int32, (tq, SB_BLOCK), 1)
           - lax.broadcasted_iota(jnp.int32, (tq, SB_BLOCK), 0))
    later = jnp.zeros((tq, SB_BLOCK), F32)
    acc = jnp.zeros((tq, SB_HEAD_DIM), F32)
    later, acc = _sb_block(q, kn_ref[0].astype(BF16), vn_ref[0].astype(BF16), later, acc, tri_ext, dmr < 0)

    def body(carry):
        i, later, acc = carry
        rows = pl.ds(pl.multiple_of((npast - 1 - i) * SB_BLOCK, SB_BLOCK), SB_BLOCK)
        later, acc = _sb_block(q, kp_ref[0, rows, :].astype(BF16), vp_ref[0, rows, :].astype(BF16),
                               later, acc, tri_ext, None)
        return i + 1, later, acc

    _, later, acc = lax.while_loop(functools.partial(_sb_more, n=npast), body, (0, later, acc))
    o_ref[0] = acc


def _sb_sample(qkv, kv_new_pad, k_past, v_past, tri_ext, layer):
    bsz, lq, _ = qkv.shape
    past = k_past.shape[1]
    assert past % SB_BLOCK == 0 and lq <= SB_BLOCK
    first = layer * bsz
    return pl.pallas_call(
        functools.partial(_sb_sample_kernel, npast=past // SB_BLOCK),
        out_shape=jax.ShapeDtypeStruct((bsz, lq, SB_WIDTH), F32),
        grid=(bsz, SB_HEADS),
        in_specs=[pl.BlockSpec((1, lq, SB_HEAD_DIM), lambda b, h: (b, 0, h)),
                  pl.BlockSpec((1, SB_BLOCK, SB_HEAD_DIM), lambda b, h: (b, 0, h)),
                  pl.BlockSpec((1, SB_BLOCK, SB_HEAD_DIM), lambda b, h: (b, 0, SB_HEADS + h)),
                  pl.BlockSpec((1, past, SB_HEAD_DIM), lambda b, h: (first + b, 0, h)),
                  pl.BlockSpec((1, past, SB_HEAD_DIM), lambda b, h: (first + b, 0, h)),
                  pl.BlockSpec((SB_BLOCK, 2 * SB_BLOCK), lambda b, h: (0, 0))],
        out_specs=pl.BlockSpec((1, lq, SB_HEAD_DIM), lambda b, h: (b, 0, h)),
        compiler_params=_cparams(("parallel", "parallel")),
        name="sb_sample",
    )(qkv, kv_new_pad, kv_new_pad, k_past, v_past, tri_ext)


def _merge_kernel(y_ref, z_ref, gate_ref, attn_ref, x_ref, nw_ref, w1_ref, w2_ref, w3_ref, gpost_ref, o_ref):
    z = z_ref[...]
    g = y_ref[...] * (z * _sigmoid(z))
    b1 = None
    for k in range(SSD_GROUPS):
        gw = slice(k * GROUP_WIDTH, (k + 1) * GROUP_WIDTH)
        gk = _rms(g[:, gw], nw_ref[:, gw])
        part = _dot(gk.astype(BF16), w1_ref[gw, :])
        b1 = part if b1 is None else b1 + part
    b2 = _dot(attn_ref[...].astype(BF16), w2_ref[...])
    gl = gate_ref[...]
    m = _sigmoid(gl[:, :D_MODEL]) * b1 + _sigmoid(gl[:, D_MODEL:]) * b2
    mixed = _dot(m.astype(BF16), w3_ref[...])
    o_ref[...] = x_ref[...] + _rms(mixed, gpost_ref[...])


def _merge(y, zg, attn, x, ssd_norm, w_br_ssd, w_br_sb, w_out, norm_post):
    m = x.shape[0]
    tm = min(256, m)
    row = lambda i: (i, 0)
    full = lambda i: (0, 0)
    return pl.pallas_call(
        _merge_kernel,
        out_shape=jax.ShapeDtypeStruct((m, D_MODEL), F32),
        grid=(m // tm,),
        in_specs=[pl.BlockSpec((tm, D_SSM), row),
                  pl.BlockSpec((tm, D_SSM), row),
                  pl.BlockSpec((tm, 2 * D_MODEL), lambda i: (i, 1)),
                  pl.BlockSpec((tm, SB_WIDTH), row),
                  pl.BlockSpec((tm, D_MODEL), row),
                  pl.BlockSpec((1, D_SSM), full),
                  pl.BlockSpec((D_SSM, D_MODEL), full),
                  pl.BlockSpec((SB_WIDTH, D_MODEL), full),
                  pl.BlockSpec((D_MODEL, D_MODEL), full),
                  pl.BlockSpec((1, D_MODEL), full)],
        out_specs=pl.BlockSpec((tm, D_MODEL), row),
        compiler_params=_cparams(("parallel",)),
        name="merge",
    )(y, zg, zg, attn, x, ssd_norm, w_br_ssd, w_br_sb, w_out, norm_post)


def _ffn_kernel(x_ref, st_ref, gpre_ref, wg_ref, wu_ref, cw_ref, cb_ref, wd_ref, gpost_ref,
                o_ref, nst_ref, h_scr, acc_scr, carry_scr):
    l = pl.program_id(1)
    j = pl.program_id(2)
    nb, tl, _ = x_ref.shape

    @pl.when(j == 0)
    def _():
        h_scr[...] = _rms(x_ref[...].reshape(nb * tl, D_MODEL), gpre_ref[...]).astype(BF16)
        acc_scr[...] = jnp.zeros_like(acc_scr)

    @pl.when(l == 0)
    def _():
        carry_scr[j] = st_ref[...]

    h = h_scr[...]
    gate = _dot(h, wg_ref[...])
    up = _dot(h, wu_ref[...])
    conv = _causal_conv(gate, carry_scr[j], cw_ref, cb_ref, FFN_CONV)
    tail = gate.reshape(nb, tl, gate.shape[1])[:, tl - SUBLANES:, :]
    carry_scr[j] = tail

    @pl.when(l == pl.num_programs(1) - 1)
    def _():
        nst_ref[...] = tail

    gelu = 0.5 * conv * (1.0 + jnp.tanh(0.7978845608028654 * (conv + 0.044715 * (conv * conv * conv))))
    acc_scr[...] += _dot((gelu * up).astype(BF16), wd_ref[...])

    @pl.when(j == pl.num_programs(2) - 1)
    def _():
        out = x_ref[...].reshape(nb * tl, D_MODEL) + _rms(acc_scr[...], gpost_ref[...])
        o_ref[...] = out.reshape(nb, tl, D_MODEL)


def _ffn(x, state8, norm_pre, w_up, conv_w, conv_b, w_down, norm_post, *, nb, tl, fc):
    bsz, length, _ = x.shape
    nf = D_FF // fc
    nl = length // tl
    full = lambda b, l, j: (0, 0)
    nst_map = lambda b, l, j: (b, 0, jnp.where(l == nl - 1, j, 0))
    return pl.pallas_call(
        _ffn_kernel,
        out_shape=(jax.ShapeDtypeStruct((bsz, length, D_MODEL), F32),
                   jax.ShapeDtypeStruct((bsz, SUBLANES, D_FF), F32)),
        grid=(bsz // nb, nl, nf),
        in_specs=[pl.BlockSpec((nb, tl, D_MODEL), lambda b, l, j: (b, l, 0)),
                  pl.BlockSpec((nb, SUBLANES, fc), lambda b, l, j: (b, 0, j)),
                  pl.BlockSpec((1, D_MODEL), full),
                  pl.BlockSpec((D_MODEL, fc), lambda b, l, j: (0, j)),
                  pl.BlockSpec((D_MODEL, fc), lambda b, l, j: (0, nf + j)),
                  pl.BlockSpec((FFN_CONV, fc), lambda b, l, j: (0, j)),
                  pl.BlockSpec((1, fc), lambda b, l, j: (0, j)),
                  pl.BlockSpec((fc, D_MODEL), lambda b, l, j: (j, 0)),
                  pl.BlockSpec((1, D_MODEL), full)],
        out_specs=(pl.BlockSpec((nb, tl, D_MODEL), lambda b, l, j: (b, l, 0)),
                   pl.BlockSpec((nb, SUBLANES, fc), nst_map)),
        scratch_shapes=[pltpu.VMEM((nb * tl, D_MODEL), BF16),
                        pltpu.VMEM((nb * tl, D_MODEL), F32),
                        pltpu.VMEM((nf, nb, SUBLANES, fc), F32)],
        compiler_params=_cparams(("parallel", "arbitrary", "arbitrary")),
        name="ffn",
    )(x, state8, norm_pre, w_up, w_up, conv_w, conv_b, w_down, norm_post)


def _ple_kernel(x_ref, p_ref, wp_ref, wg_ref, g_ref, o_ref):
    x = x_ref[...]
    ple = _dot(p_ref[...].astype(BF16), wp_ref[...]) * _sigmoid(_dot(x.astype(BF16), wg_ref[...]))
    o_ref[...] = x + _rms(ple, g_ref[...])


def _ple(x, p, w_ple, w_gate, gain):
    m = x.shape[0]
    tm = min(512, m)
    row = lambda i: (i, 0)
    full = lambda i: (0, 0)
    return pl.pallas_call(
        _ple_kernel,
        out_shape=jax.ShapeDtypeStruct((m, D_MODEL), F32),
        grid=(m // tm,),
        in_specs=[pl.BlockSpec((tm, D_MODEL), row),
                  pl.BlockSpec((tm, PLE_DIM), row),
                  pl.BlockSpec((PLE_DIM, D_MODEL), full),
                  pl.BlockSpec((D_MODEL, D_MODEL), full),
                  pl.BlockSpec((1, D_MODEL), full)],
        out_specs=pl.BlockSpec((tm, D_MODEL), row),
        compiler_params=_cparams(("parallel",)),
        name="ple",
    )(x, p, w_ple, w_gate, gain)


def _pad_rows_to8(state):
    return jnp.pad(state, ((0, 0), (SUBLANES - state.shape[1], 0), (0, 0)))


def _pad_lanes(v):
    return jnp.pad(v, (0, LANES - v.shape[0]))[None, :]


def _constants(cl):
    head = jnp.arange(LANES)[:, None]
    chan = jnp.arange(D_SSM)[None, :] // SSD_HEAD_DIM
    expand = (head == chan).astype(BF16)
    idx = jnp.arange(cl)
    tri_cum = (idx[None, :] <= idx[:, None]).astype(BF16)
    kidx = jnp.arange(SB_BLOCK)
    tri_in = (kidx[:, None] > kidx[None, :]).astype(BF16)
    tri_ext = jnp.concatenate([tri_in, jnp.ones((SB_BLOCK, SB_BLOCK), BF16)], axis=1)
    return expand, tri_cum, tri_ext


def _layer(x, p, conv_st, ssm_st, k_cache, v_cache, ffn_st, lw, layer):
    bsz, length, _ = x.shape
    m = bsz * length
    x2 = x.reshape(m, D_MODEL)
    cl = CHUNK if length >= CHUNK else length
    expand, tri_cum, tri_ext = _constants(cl)

    w_in = lw['w_in']
    o_z, o_xbc, o_dt, o_q, o_g = 0, D_SSM, D_SSM + SSD_CONV_DIM, D_SSM + SSD_CONV_DIM + SSD_HEADS, \
        D_SSM + SSD_CONV_DIM + SSD_HEADS + 3 * SB_WIDTH
    w_zg = jnp.concatenate([w_in[:, o_z:o_xbc], w_in[:, o_g:]], axis=1).astype(BF16)
    w_xbc = w_in[:, o_xbc:o_dt].astype(BF16)
    w_dt = jnp.pad(w_in[:, o_dt:o_q], ((0, 0), (0, LANES - SSD_HEADS))).astype(BF16)
    w_qkv = w_in[:, o_q:o_g].astype(BF16)
    g_pre = lw['norm_pre_mix'][None, :]
    zg = _norm_matmul(x2, g_pre, w_zg)
    xbc = _norm_matmul(x2, g_pre, w_xbc)
    dt_raw = _norm_matmul(x2, g_pre, w_dt)
    qkv = _norm_matmul(x2, g_pre, w_qkv)

    xbc3 = xbc.reshape(bsz, length, SSD_CONV_DIM)
    tl = min(256, length)
    y, h_new = _ssd(xbc3, dt_raw.reshape(bsz, length, LANES), _pad_rows_to8(conv_st),
                    lw['ssd_conv_w'], lw['ssd_conv_b'][None, :], _pad_lanes(lw['ssd_dt_bias']),
                    _pad_lanes(lw['ssd_a_log']), jnp.repeat(lw['ssd_d'], SSD_HEAD_DIM)[None, :],
                    ssm_st.reshape(bsz, D_SSM, SSD_STATE), expand, tri_cum, tl=tl, cl=cl)
    new_conv = xbc3[:, length - (SSD_CONV - 1):]
    new_ssm = h_new.reshape(bsz, SSD_HEADS, SSD_HEAD_DIM, SSD_STATE)

    qkv3 = qkv.reshape(bsz, length, 3 * SB_WIDTH)
    kv = qkv3[:, :, SB_WIDTH:]
    k_new = kv[:, :, :SB_WIDTH].reshape(bsz, length, SB_HEADS, SB_HEAD_DIM)
    v_new = kv[:, :, SB_WIDTH:].reshape(bsz, length, SB_HEADS, SB_HEAD_DIM)
    if k_cache is None:
        attn = _sb_prompt(qkv3, kv.astype(BF16), tri_ext, tq=2 * SB_BLOCK)
    else:
        depth, _, past = k_cache.shape[:3]
        kv_pad = jnp.pad(kv, ((0, 0), (0, SB_BLOCK - length), (0, 0)))
        attn = _sb_sample(qkv3, kv_pad, k_cache.reshape(depth * bsz, past, SB_WIDTH),
                          v_cache.reshape(depth * bsz, past, SB_WIDTH), tri_ext, layer)

    x2 = _merge(y.reshape(m, D_SSM), zg, attn.reshape(m, SB_WIDTH), x2, lw['ssd_norm'][None, :],
                lw['w_br_ssd'].astype(BF16), lw['w_br_sb'].astype(BF16), lw['w_out'].astype(BF16),
                lw['norm_post_mix'][None, :])

    x3, ffn8 = _ffn(x2.reshape(bsz, length, D_MODEL), _pad_rows_to8(ffn_st), lw['norm_pre_ffn'][None, :],
                    lw['w_up'].astype(BF16), lw['ffn_conv_w'], lw['ffn_conv_b'][None, :],
                    lw['w_down'].astype(BF16), lw['norm_post_ffn'][None, :],
                    nb=max(1, min(bsz, FFN_ROWS // length)), tl=min(FFN_ROWS, length), fc=512)
    new_ffn = ffn8[:, SUBLANES - (FFN_CONV - 1):]

    x2 = _ple(x3.reshape(m, D_MODEL), p.reshape(m, PLE_DIM), lw['w_ple'].astype(BF16),
              lw['w_ple_gate'].astype(BF16), lw['norm_ple'][None, :])
    return x2.reshape(bsz, length, D_MODEL), (new_conv, new_ssm, k_new, v_new, new_ffn)


_LAYER_WEIGHTS = ('norm_pre_mix', 'w_in', 'ssd_conv_w', 'ssd_conv_b', 'ssd_dt_bias', 'ssd_a_log', 'ssd_d',
                  'ssd_norm', 'w_br_ssd', 'w_br_sb', 'w_out', 'norm_post_mix', 'norm_pre_ffn', 'w_up',
                  'ffn_conv_w', 'ffn_conv_b', 'w_down', 'norm_post_ffn', 'w_ple', 'w_ple_gate', 'norm_ple')


def _run(x, p, conv0, ssm0, k0, v0, ffn0, weights):
    per_layer = []
    for i in range(DEPTH):
        lw = {name: weights[name][i] for name in _LAYER_WEIGHTS}
        x, st = _layer(x, p[i], conv0[i], ssm0[i], k0, v0, ffn0[i], lw, i)
        per_layer.append(st)
    return x, [jnp.stack([st[j] for st in per_layer]) for j in range(5)]


def kernel(x_prompt, x_sample, state_ssd_conv, state_ssd, cache_sb_k, cache_sb_v, state_ffn_conv, p_prompt, p_sample, norm_pre_mix, w_in, ssd_conv_w, ssd_conv_b, ssd_dt_bias, ssd_a_log, ssd_d, ssd_norm, w_br_ssd, w_br_sb, w_out, norm_post_mix, norm_pre_ffn, w_up, ffn_conv_w, ffn_conv_b, w_down, norm_post_ffn, w_ple, w_ple_gate, norm_ple):
    weights = dict(norm_pre_mix=norm_pre_mix, w_in=w_in, ssd_conv_w=ssd_conv_w, ssd_conv_b=ssd_conv_b,
                   ssd_dt_bias=ssd_dt_bias, ssd_a_log=ssd_a_log, ssd_d=ssd_d, ssd_norm=ssd_norm,
                   w_br_ssd=w_br_ssd, w_br_sb=w_br_sb, w_out=w_out, norm_post_mix=norm_post_mix,
                   norm_pre_ffn=norm_pre_ffn, w_up=w_up, ffn_conv_w=ffn_conv_w, ffn_conv_b=ffn_conv_b,
                   w_down=w_down, norm_post_ffn=norm_post_ffn, w_ple=w_ple, w_ple_gate=w_ple_gate,
                   norm_ple=norm_ple)
    bp = x_prompt.shape[0]
    zero_conv = jnp.zeros((DEPTH, bp, SSD_CONV - 1, SSD_CONV_DIM), F32)
    zero_ssm = jnp.zeros((DEPTH, bp, SSD_HEADS, SSD_HEAD_DIM, SSD_STATE), F32)
    zero_ffn = jnp.zeros((DEPTH, bp, FFN_CONV - 1, D_FF), F32)
    y_prompt, ps = _run(x_prompt, p_prompt, zero_conv, zero_ssm, None, None, zero_ffn, weights)
    y_sample, ss = _run(x_sample, p_sample, state_ssd_conv, state_ssd, cache_sb_k, cache_sb_v,
                        state_ffn_conv, weights)
    return (y_prompt, y_sample, ps[0], ps[1], ps[2], ps[3], ps[4], ss[0], ss[1], ss[2], ss[3], ss[4])
```

```python
import functools

import jax
import jax.numpy as jnp
from jax import lax
from jax.experimental import pallas as pl
from jax.experimental.pallas import tpu as pltpu

F32 = jnp.float32
BF16 = jnp.bfloat16

D_MODEL = 1024
DEPTH = 2
CHUNK = 64
D_SSM = 2048
SSD_HEAD_DIM = 64
SSD_HEADS = 32
SSD_GROUPS = 4
HEADS_PER_GROUP = 8
SSD_STATE = 128
SSD_CONV = 4
SSD_CONV_DIM = D_SSM + 2 * SSD_GROUPS * SSD_STATE
GROUP_WIDTH = D_SSM // SSD_GROUPS
SB_HEAD_DIM = 128
SB_HEADS = 8
SB_WIDTH = 1024
SB_BLOCK = 128
D_FF = 4096
FFN_CONV = 3
PLE_DIM = 256
EPS = 1e-6
SB_SCALE = SB_HEAD_DIM ** -0.5
SB_HEAD_GROUP = 8
SB_DEAD_LOG = 120.0

LANES = 128
SUBLANES = 8
VMEM_LIMIT = 56 * 1024 * 1024
FFN_ROWS = 1024


def _cparams(sem):
    return pltpu.CompilerParams(dimension_semantics=sem, vmem_limit_bytes=VMEM_LIMIT)


def _rms(x, gain):
    return x * lax.rsqrt(jnp.mean(x * x, axis=-1, keepdims=True) + EPS) * gain


def _sigmoid(x):
    return 1.0 / (1.0 + jnp.exp(-x))


def _dot(a, b):
    return jnp.dot(a, b, preferred_element_type=F32)


def _dot_nt(a, b):
    return lax.dot_general(a, b, (((1,), (1,)), ((), ())), preferred_element_type=F32)


def _dot_tn(a, b):
    return lax.dot_general(a, b, (((0,), (0,)), ((), ())), preferred_element_type=F32)


def _split_bf16(v, n):
    parts = []
    r = v
    for _ in range(n):
        p = r.astype(BF16)
        parts.append(p)
        r = r - p.astype(F32)
    return parts


def _shifted_rows(x, hist, k):
    nseq = hist.shape[0]
    tl = x.shape[0] // nseq
    rolled = pltpu.roll(x, k, 0)
    row = lax.broadcasted_iota(jnp.int32, (SUBLANES, x.shape[1]), 0)
    pieces = []
    for b in range(nseq):
        start = b * tl
        pieces.append(jnp.where(row < k, pltpu.roll(hist[b], k, 0), rolled[start:start + SUBLANES]))
        if tl > SUBLANES:
            pieces.append(rolled[start + SUBLANES:start + tl])
    return pieces[0] if len(pieces) == 1 else jnp.concatenate(pieces, axis=0)


def _causal_conv(x, hist, w_ref, b_ref, width):
    acc = b_ref[...]
    for tap in range(width):
        k = width - 1 - tap
        sh = x if k == 0 else _shifted_rows(x, hist, k)
        acc = acc + sh * w_ref[tap:tap + 1, :]
    return acc


def _norm_matmul_kernel(x_ref, g_ref, w_ref, o_ref, h_ref, *, normalize):
    @pl.when(pl.program_id(1) == 0)
    def _():
        x = x_ref[...]
        if normalize:
            x = _rms(x, g_ref[...])
        h_ref[...] = x.astype(BF16)

    o_ref[...] = _dot(h_ref[...], w_ref[...])


def _norm_matmul(x, gain, w, *, normalize=True):
    m, k = x.shape
    n = w.shape[1]
    tm = min(1024, m)
    tn = min(1024, n)
    return pl.pallas_call(
        functools.partial(_norm_matmul_kernel, normalize=normalize),
        out_shape=jax.ShapeDtypeStruct((m, n), F32),
        grid=(m // tm, n // tn),
        in_specs=[pl.BlockSpec((tm, k), lambda i, j: (i, 0)),
                  pl.BlockSpec((1, k), lambda i, j: (0, 0)),
                  pl.BlockSpec((k, tn), lambda i, j: (0, j))],
        out_specs=pl.BlockSpec((tm, tn), lambda i, j: (i, j)),
        scratch_shapes=[pltpu.VMEM((tm, k), BF16)],
        compiler_params=_cparams(("parallel", "arbitrary")),
        name="norm_matmul",
    )(x, gain, w)


PROJ_TN = 512
_PROJ_WIDTHS = (2 * D_SSM, SSD_CONV_DIM, SB_WIDTH, SB_WIDTH, SB_WIDTH)
_PROJ_FIRST = tuple(sum(_PROJ_WIDTHS[:n]) // PROJ_TN for n in range(len(_PROJ_WIDTHS) + 1))


def _in_proj_kernel(x_ref, g_ref, w_ref, zg_ref, xbc_ref, q_ref, k_ref, v_ref, kvb_ref, h_ref):
    j = pl.program_id(1)

    @pl.when(j == 0)
    def _():
        h_ref[...] = _rms(x_ref[...], g_ref[...]).astype(BF16)

    def owns(n):
        return jnp.logical_and(j >= _PROJ_FIRST[n], j < _PROJ_FIRST[n + 1])

    def proj():
        return _dot(h_ref[...], w_ref[...])

    @pl.when(owns(0))
    def _():
        zg_ref[...] = proj()

    @pl.when(owns(1))
    def _():
        xbc_ref[...] = proj()

    @pl.when(owns(2))
    def _():
        q_ref[...] = proj().astype(BF16)

    heads_per_tile = PROJ_TN // SB_HEAD_DIM
    tm = x_ref.shape[0]
    for n, rows_ref in ((3, k_ref), (4, v_ref)):
        for t in range(_PROJ_FIRST[n + 1] - _PROJ_FIRST[n]):
            @pl.when(j == _PROJ_FIRST[n] + t)
            def _(rows_ref=rows_ref, t=t):
                r = proj()
                kvb_ref[...] = r.astype(BF16)
                for hh in range(heads_per_tile):
                    head = t * heads_per_tile + hh
                    rows_ref[pl.ds(head, tm, stride=SB_HEADS), :] = r[:, hh * SB_HEAD_DIM:(hh + 1) * SB_HEAD_DIM]


def _in_proj(x, gain, w):
    m, k = x.shape
    tm = min(1024, m)
    tn = PROJ_TN
    f32 = lambda n: jax.ShapeDtypeStruct((m, n), F32)
    head_rows = jax.ShapeDtypeStruct((m * SB_HEADS, SB_HEAD_DIM), F32)

    def tile_of(n_first, n_last):
        first, count = _PROJ_FIRST[n_first], _PROJ_FIRST[n_last + 1] - _PROJ_FIRST[n_first]
        return pl.BlockSpec((tm, tn), lambda i, j: (i, jnp.clip(j - first, 0, count - 1)))

    rows_spec = pl.BlockSpec((tm * SB_HEADS, SB_HEAD_DIM), lambda i, j: (i, 0))
    return pl.pallas_call(
        _in_proj_kernel,
        out_shape=(f32(2 * D_SSM), f32(SSD_CONV_DIM), jax.ShapeDtypeStruct((m, SB_WIDTH), BF16),
                   head_rows, head_rows, jax.ShapeDtypeStruct((m, 2 * SB_WIDTH), BF16)),
        grid=(m // tm, _PROJ_FIRST[-1]),
        in_specs=[pl.BlockSpec((tm, k), lambda i, j: (i, 0)),
                  pl.BlockSpec((1, k), lambda i, j: (0, 0)),
                  pl.BlockSpec((k, tn), lambda i, j: (0, j))],
        out_specs=(tile_of(0, 0), tile_of(1, 1), tile_of(2, 2), rows_spec, rows_spec, tile_of(3, 4)),
        scratch_shapes=[pltpu.VMEM((tm, k), BF16)],
        compiler_params=_cparams(("parallel", "arbitrary")),
        name="in_proj",
    )(x, gain, w)


def _ssd_kernel(xbc_ref, dt_ref, cst_ref, cw_ref, cb_ref, dtb_ref, alog_ref, dexp_ref, h0_ref, e_ref, tri_ref,
                y_ref, hout_ref, h_scr, carry_scr, xc_scr, dt_scr, *, cl, nchunk):
    l = pl.program_id(1)

    @pl.when(l == 0)
    def _():
        h_scr[...] = h0_ref[0]
        carry_scr[...] = cst_ref[0]

    xt = xbc_ref[0]
    tl = xt.shape[0]
    conv = _causal_conv(xt, carry_scr[...][None], cw_ref, cb_ref, SSD_CONV)
    carry_scr[...] = xt[tl - SUBLANES:tl]
    xc_scr[...] = conv * _sigmoid(conv)
    dtr = dt_ref[0] + dtb_ref[...]
    dt_scr[...] = jnp.maximum(dtr, 0.0) + jnp.log1p(jnp.exp(-jnp.abs(dtr)))

    a_head = -jnp.exp(alog_ref[...])
    tri = tri_ref[...]
    expand = e_ref[...]
    rowi = lax.broadcasted_iota(jnp.int32, (cl, cl), 0)
    coli = lax.broadcasted_iota(jnp.int32, (cl, cl), 1)
    causal = coli <= rowi
    lane = lax.broadcasted_iota(jnp.int32, (1, LANES), 1)
    half_masks = ((lane < SSD_HEAD_DIM).astype(F32), (lane >= SSD_HEAD_DIM).astype(F32))

    def expand_heads(v):
        hi, lo = _split_bf16(v, 2)
        return _dot(hi, expand) + _dot(lo, expand)

    def chunk(r0):
        rows = pl.ds(r0, cl)
        xs = xc_scr[rows, 0:D_SSM]
        bm = xc_scr[rows, D_SSM:D_SSM + SSD_GROUPS * SSD_STATE]
        cm = xc_scr[rows, D_SSM + SSD_GROUPS * SSD_STATE:SSD_CONV_DIM]
        dtc = dt_scr[rows, :]
        a = dtc * a_head
        acum = sum(_dot(tri, p) for p in _split_bf16(a, 3))
        if cl < LANES:
            acum_sq = jnp.concatenate([acum, jnp.zeros((LANES - cl, LANES), F32)], axis=0)
        else:
            acum_sq = acum
        acum_t = acum_sq.T
        a_end = acum[cl - 1:cl, :]
        dt_e = expand_heads(dtc)
        dec_end_e = expand_heads(jnp.exp(a_end - acum))
        dec_in_e = expand_heads(jnp.exp(acum))
        chunk_decay = jnp.broadcast_to(jnp.exp(acum_t[:, cl - 1:cl]), (LANES, LANES))
        x_dt = xs * dt_e
        x_end = x_dt * dec_end_e
        for g in range(SSD_GROUPS):
            gs = slice(g * SSD_STATE, (g + 1) * SSD_STATE)
            gw = slice(g * GROUP_WIDTH, (g + 1) * GROUP_WIDTH)
            bg = bm[:, gs].astype(BF16)
            cg = cm[:, gs].astype(BF16)
            cb = _dot_nt(cg, bg)
            h_prev = h_scr[gw, :]
            y_off = _dot_nt(cg, h_prev.astype(BF16)) * dec_in_e[:, gw]
            st = _dot_tn(x_end[:, gw].astype(BF16), bg)
            for pr in range(HEADS_PER_GROUP // 2):
                ps = slice(g * GROUP_WIDTH + pr * LANES, g * GROUP_WIDTH + (pr + 1) * LANES)
                xp = x_dt[:, ps]
                yp = y_off[:, pr * LANES:(pr + 1) * LANES]
                for sub in range(2):
                    e = g * HEADS_PER_GROUP + pr * 2 + sub
                    seg = acum[:, e:e + 1] - acum_t[e:e + 1, 0:cl]
                    dec = jnp.exp(jnp.where(causal, seg, -jnp.inf))
                    m_e = (cb * dec).astype(BF16)
                    yp = yp + _dot(m_e, (xp * half_masks[sub]).astype(BF16))
                    hs = slice(e * SSD_HEAD_DIM, (e + 1) * SSD_HEAD_DIM)
                    ls = slice((pr * 2 + sub) * SSD_HEAD_DIM, (pr * 2 + sub + 1) * SSD_HEAD_DIM)
                    h_scr[hs, :] = h_scr[hs, :] * chunk_decay[e:e + 1, :] + st[ls, :]
                y_ref[0, rows, ps] = yp + dexp_ref[:, ps] * xs[:, ps]

    if nchunk == 1:
        chunk(0)
    else:
        def body(c, carry):
            chunk(pl.multiple_of(c * cl, cl))
            return carry
        lax.fori_loop(0, nchunk, body, 0)

    @pl.when(l == pl.num_programs(1) - 1)
    def _():
        hout_ref[0] = h_scr[...]


def _ssd(xbc, dt_raw, conv_state8, conv_w, conv_b, dt_bias, a_log, d_exp, h0, expand, tri, *, tl, cl):
    bsz, length, _ = xbc.shape
    kernel = functools.partial(_ssd_kernel, cl=cl, nchunk=tl // cl)
    full2 = lambda b, l: (0, 0)
    return pl.pallas_call(
        kernel,
        out_shape=(jax.ShapeDtypeStruct((bsz, length, D_SSM), F32),
                   jax.ShapeDtypeStruct((bsz, D_SSM, SSD_STATE), F32)),
        grid=(bsz, length // tl),
        in_specs=[pl.BlockSpec((1, tl, SSD_CONV_DIM), lambda b, l: (b, l, 0)),
                  pl.BlockSpec((1, tl, LANES), lambda b, l: (b, l, 0)),
                  pl.BlockSpec((1, SUBLANES, SSD_CONV_DIM), lambda b, l: (b, 0, 0)),
                  pl.BlockSpec((SSD_CONV, SSD_CONV_DIM), full2),
                  pl.BlockSpec((1, SSD_CONV_DIM), full2),
                  pl.BlockSpec((1, LANES), full2),
                  pl.BlockSpec((1, LANES), full2),
                  pl.BlockSpec((1, D_SSM), full2),
                  pl.BlockSpec((1, D_SSM, SSD_STATE), lambda b, l: (b, 0, 0)),
                  pl.BlockSpec((LANES, D_SSM), full2),
                  pl.BlockSpec((cl, cl), full2)],
        out_specs=(pl.BlockSpec((1, tl, D_SSM), lambda b, l: (b, l, 0)),
                   pl.BlockSpec((1, D_SSM, SSD_STATE), lambda b, l: (b, 0, 0))),
        scratch_shapes=[pltpu.VMEM((D_SSM, SSD_STATE), F32),
                        pltpu.VMEM((SUBLANES, SSD_CONV_DIM), F32),
                        pltpu.VMEM((tl, SSD_CONV_DIM), F32),
                        pltpu.VMEM((tl, LANES), F32)],
        compiler_params=_cparams(("parallel", "arbitrary")),
        name="ssd",
    )(xbc, dt_raw, conv_state8, conv_w, conv_b, dt_bias, a_log, d_exp, h0, expand, tri)


def _sb_blocks(qs, kbs, vbs, laters, accs, tri_ext, earlier):
    zs = [_dot_nt(q, kb) * SB_SCALE for q, kb in zip(qs, kbs)]
    log_keeps, log_betas = [], []
    for z in zs:
        l1p = jnp.log(1.0 + jnp.exp(-jnp.abs(z)))
        log_keep = -(jnp.maximum(z, 0.0) + l1p)
        if earlier is not None:
            log_keep = jnp.where(earlier, log_keep, 0.0)
        log_keeps.append(log_keep)
        log_betas.append(jnp.minimum(z, 0.0) - l1p)
    exts = [_dot(lk.astype(BF16), tri_ext) for lk in log_keeps]
    ws = []
    for log_beta, ext, later in zip(log_betas, exts, laters):
        w = jnp.exp(log_beta + ext[:, :SB_BLOCK] + later)
        if earlier is not None:
            w = jnp.where(earlier, w, 0.0)
        ws.append(w.astype(BF16))
    accs = [acc + _dot(w, vb) for acc, w, vb in zip(accs, ws, vbs)]
    laters = [later + ext[:, SB_BLOCK:] for later, ext in zip(laters, exts)]
    return laters, accs


def _sb_cols(h):
    return slice(h * SB_HEAD_DIM, (h + 1) * SB_HEAD_DIM)


def _sb_alive(later_scr):
    return jnp.max(later_scr[...]) > -SB_DEAD_LOG


def _sb_heads(q_ref, blocks, later_scr, acc_scr, tri_ext, init):
    tq = q_ref.shape[1]
    for h0 in range(0, SB_HEADS, SB_HEAD_GROUP):
        heads = range(h0, h0 + SB_HEAD_GROUP)
        qs = [q_ref[0, :, _sb_cols(h)] for h in heads]
        if init:
            laters = [jnp.zeros((tq, SB_BLOCK), F32) for _ in heads]
            accs = [jnp.zeros((tq, SB_HEAD_DIM), F32) for _ in heads]
        else:
            laters = [later_scr[h] for h in heads]
            accs = [acc_scr[h] for h in heads]
        for k_of, v_of, earlier in blocks:
            laters, accs = _sb_blocks(qs, [k_of(h) for h in heads], [v_of(h) for h in heads],
                                      laters, accs, tri_ext, earlier)
        for h, later, acc in zip(heads, laters, accs):
            later_scr[h] = later
            acc_scr[h] = acc


def _sb_kv_of(read):
    return (lambda h: read(_sb_cols(h))), (lambda h: read(_sb_cols(SB_HEADS + h)))


def _sb_diag_mask(tq):
    key = lax.broadcasted_iota(jnp.int32, (tq, SB_BLOCK), 1)
    query = lax.broadcasted_iota(jnp.int32, (tq, SB_BLOCK), 0)
    return key < query


def _sb_prompt_kernel(q_ref, kv0_ref, kv1_ref, kv2_ref, kv_hbm, tri_ref, o_ref,
                      later_scr, acc_scr, kv_buf, sem):
    b = pl.program_id(0)
    qi = pl.program_id(1)
    tri_ext = tri_ref[...]
    state = (later_scr, acc_scr, tri_ext)

    k0, v0 = _sb_kv_of(lambda c: kv0_ref[0, :, c])
    _sb_heads(q_ref, [(k0, v0, _sb_diag_mask(q_ref.shape[1]))], *state, init=True)

    @pl.when(qi >= 2)
    def _():
        k1, v1 = _sb_kv_of(lambda c: kv1_ref[0, :, c])
        k2, v2 = _sb_kv_of(lambda c: kv2_ref[0, :, c])
        _sb_heads(q_ref, [(k1, v1, None), (k2, v2, None)], *state, init=False)

    def more(carry):
        j, alive = carry
        return jnp.logical_and(j >= 0, alive)

    def body(carry):
        j, _ = carry
        rows = pl.ds(pl.multiple_of(j * SB_BLOCK, SB_BLOCK), SB_BLOCK)
        copy = pltpu.make_async_copy(kv_hbm.at[b, rows, :], kv_buf, sem)
        copy.start()
        copy.wait()
        kb, vb = _sb_kv_of(lambda c: kv_buf[:, c])
        _sb_heads(q_ref, [(kb, vb, None)], *state, init=False)
        return j - 1, _sb_alive(later_scr)

    lax.while_loop(more, body, (jnp.where(qi >= 2, qi - 3, qi - 1), _sb_alive(later_scr)))
    for h in range(SB_HEADS):
        o_ref[0, :, _sb_cols(h)] = acc_scr[h]


def _sb_prompt(q, kvb, tri_ext):
    bsz, length, _ = q.shape
    tq = SB_BLOCK
    blk = lambda back: pl.BlockSpec((1, SB_BLOCK, 2 * SB_WIDTH), lambda b, i: (b, jnp.maximum(i - back, 0), 0))
    return pl.pallas_call(
        _sb_prompt_kernel,
        out_shape=jax.ShapeDtypeStruct((bsz, length, SB_WIDTH), F32),
        grid=(bsz, length // tq),
        in_specs=[pl.BlockSpec((1, tq, SB_WIDTH), lambda b, i: (b, i, 0)),
                  blk(0), blk(1), blk(2),
                  pl.BlockSpec(memory_space=pl.ANY),
                  pl.BlockSpec((SB_BLOCK, 2 * SB_BLOCK), lambda b, i: (0, 0))],
        out_specs=pl.BlockSpec((1, tq, SB_WIDTH), lambda b, i: (b, i, 0)),
        scratch_shapes=[pltpu.VMEM((SB_HEADS, tq, SB_BLOCK), F32),
                        pltpu.VMEM((SB_HEADS, tq, SB_HEAD_DIM), F32),
                        pltpu.VMEM((SB_BLOCK, 2 * SB_WIDTH), BF16),
                        pltpu.SemaphoreType.DMA(())],
        compiler_params=_cparams(("parallel", "arbitrary")),
        name="sb_prompt",
    )(q, kvb, kvb, kvb, kvb, tri_ext)


def _sb_sample_kernel(q_ref, kvn_ref, kp_ref, vp_ref, tri_ref, o_ref, later_scr, acc_scr, *, npast):
    tri_ext = tri_ref[...]
    state = (later_scr, acc_scr, tri_ext)
    kn, vn = _sb_kv_of(lambda c: kvn_ref[0, :, c])
    _sb_heads(q_ref, [(kn, vn, _sb_diag_mask(q_ref.shape[1]))], *state, init=True)

    def more(carry):
        j, alive = carry
        return jnp.logical_and(j >= 0, alive)

    def body(carry):
        j, _ = carry
        base = pl.multiple_of(j * (SB_BLOCK * SB_HEADS), SB_BLOCK * SB_HEADS)
        kb = lambda h: kp_ref[0, pl.ds(base + h, SB_BLOCK, stride=SB_HEADS), :].astype(BF16)
        vb = lambda h: vp_ref[0, pl.ds(base + h, SB_BLOCK, stride=SB_HEADS), :].astype(BF16)
        _sb_heads(q_ref, [(kb, vb, None)], *state, init=False)
        return j - 1, _sb_alive(later_scr)

    lax.while_loop(more, body, (npast - 1, _sb_alive(later_scr)))
    for h in range(SB_HEADS):
        o_ref[0, :, _sb_cols(h)] = acc_scr[h]


def _sb_sample(q, kvb_new_pad, k_past, v_past, tri_ext, layer):
    bsz, lq, _ = q.shape
    rows = k_past.shape[1]
    past = rows // SB_HEADS
    assert past % SB_BLOCK == 0 and lq <= SB_BLOCK
    first = layer * bsz
    return pl.pallas_call(
        functools.partial(_sb_sample_kernel, npast=past // SB_BLOCK),
        out_shape=jax.ShapeDtypeStruct((bsz, lq, SB_WIDTH), F32),
        grid=(bsz,),
        in_specs=[pl.BlockSpec((1, lq, SB_WIDTH), lambda b: (b, 0, 0)),
                  pl.BlockSpec((1, SB_BLOCK, 2 * SB_WIDTH), lambda b: (b, 0, 0)),
                  pl.BlockSpec((1, rows, SB_HEAD_DIM), lambda b: (first + b, 0, 0)),
                  pl.BlockSpec((1, rows, SB_HEAD_DIM), lambda b: (first + b, 0, 0)),
                  pl.BlockSpec((SB_BLOCK, 2 * SB_BLOCK), lambda b: (0, 0))],
        out_specs=pl.BlockSpec((1, lq, SB_WIDTH), lambda b: (b, 0, 0)),
        scratch_shapes=[pltpu.VMEM((SB_HEADS, lq, SB_BLOCK), F32),
                        pltpu.VMEM((SB_HEADS, lq, SB_HEAD_DIM), F32)],
        compiler_params=_cparams(("parallel",)),
        name="sb_sample",
    )(q, kvb_new_pad, k_past, v_past, tri_ext)


def _merge_kernel(y_ref, z_ref, gate_ref, attn_ref, x_ref, nw_ref, w1_ref, w2_ref, w3_ref, gpost_ref, o_ref):
    z = z_ref[...]
    g = y_ref[...] * (z * _sigmoid(z))
    b1 = None
    for k in range(SSD_GROUPS):
        gw = slice(k * GROUP_WIDTH, (k + 1) * GROUP_WIDTH)
        gk = _rms(g[:, gw], nw_ref[:, gw])
        part = _dot(gk.astype(BF16), w1_ref[gw, :])
        b1 = part if b1 is None else b1 + part
    b2 = _dot(attn_ref[...].astype(BF16), w2_ref[...])
    gl = gate_ref[...]
    m = _sigmoid(gl[:, :D_MODEL]) * b1 + _sigmoid(gl[:, D_MODEL:]) * b2
    mixed = _dot(m.astype(BF16), w3_ref[...])
    o_ref[...] = x_ref[...] + _rms(mixed, gpost_ref[...])


def _merge(y, zg, attn, x, ssd_norm, w_br_ssd, w_br_sb, w_out, norm_post):
    m = x.shape[0]
    tm = min(256, m)
    row = lambda i: (i, 0)
    full = lambda i: (0, 0)
    return pl.pallas_call(
        _merge_kernel,
        out_shape=jax.ShapeDtypeStruct((m, D_MODEL), F32),
        grid=(m // tm,),
        in_specs=[pl.BlockSpec((tm, D_SSM), row),
                  pl.BlockSpec((tm, D_SSM), row),
                  pl.BlockSpec((tm, 2 * D_MODEL), lambda i: (i, 1)),
                  pl.BlockSpec((tm, SB_WIDTH), row),
                  pl.BlockSpec((tm, D_MODEL), row),
                  pl.BlockSpec((1, D_SSM), full),
                  pl.BlockSpec((D_SSM, D_MODEL), full),
                  pl.BlockSpec((SB_WIDTH, D_MODEL), full),
                  pl.BlockSpec((D_MODEL, D_MODEL), full),
                  pl.BlockSpec((1, D_MODEL), full)],
        out_specs=pl.BlockSpec((tm, D_MODEL), row),
        compiler_params=_cparams(("parallel",)),
        name="merge",
    )(y, zg, zg, attn, x, ssd_norm, w_br_ssd, w_br_sb, w_out, norm_post)


def _ffn_kernel(x_ref, st_ref, gpre_ref, wg_ref, wu_ref, cw_ref, cb_ref, wd_ref, gpost_ref,
                o_ref, nst_ref, h_scr, acc_scr, carry_scr):
    l = pl.program_id(1)
    j = pl.program_id(2)
    nb, tl, _ = x_ref.shape

    @pl.when(j == 0)
    def _():
        h_scr[...] = _rms(x_ref[...].reshape(nb * tl, D_MODEL), gpre_ref[...]).astype(BF16)
        acc_scr[...] = jnp.zeros_like(acc_scr)

    @pl.when(l == 0)
    def _():
        carry_scr[j] = st_ref[...]

    h = h_scr[...]
    gate = _dot(h, wg_ref[...])
    up = _dot(h, wu_ref[...])
    conv = _causal_conv(gate, carry_scr[j], cw_ref, cb_ref, FFN_CONV)
    tail = gate.reshape(nb, tl, gate.shape[1])[:, tl - SUBLANES:, :]
    carry_scr[j] = tail

    @pl.when(l == pl.num_programs(1) - 1)
    def _():
        nst_ref[...] = tail

    gelu = 0.5 * conv * (1.0 + jnp.tanh(0.7978845608028654 * (conv + 0.044715 * (conv * conv * conv))))
    acc_scr[...] += _dot((gelu * up).astype(BF16), wd_ref[...])

    @pl.when(j == pl.num_programs(2) - 1)
    def _():
        out = x_ref[...].reshape(nb * tl, D_MODEL) + _rms(acc_scr[...], gpost_ref[...])
        o_ref[...] = out.reshape(nb, tl, D_MODEL)


def _ffn(x, state8, norm_pre, w_up, conv_w, conv_b, w_down, norm_post, *, nb, tl, fc):
    bsz, length, _ = x.shape
    nf = D_FF // fc
    nl = length // tl
    full = lambda b, l, j: (0, 0)
    nst_map = lambda b, l, j: (b, 0, jnp.where(l == nl - 1, j, 0))
    return pl.pallas_call(
        _ffn_kernel,
        out_shape=(jax.ShapeDtypeStruct((bsz, length, D_MODEL), F32),
                   jax.ShapeDtypeStruct((bsz, SUBLANES, D_FF), F32)),
        grid=(bsz // nb, nl, nf),
        in_specs=[pl.BlockSpec((nb, tl, D_MODEL), lambda b, l, j: (b, l, 0)),
                  pl.BlockSpec((nb, SUBLANES, fc), lambda b, l, j: (b, 0, j)),
                  pl.BlockSpec((1, D_MODEL), full),
                  pl.BlockSpec((D_MODEL, fc), lambda b, l, j: (0, j)),
                  pl.BlockSpec((D_MODEL, fc), lambda b, l, j: (0, nf + j)),
                  pl.BlockSpec((FFN_CONV, fc), lambda b, l, j: (0, j)),
                  pl.BlockSpec((1, fc), lambda b, l, j: (0, j)),
                  pl.BlockSpec((fc, D_MODEL), lambda b, l, j: (j, 0)),
                  pl.BlockSpec((1, D_MODEL), full)],
        out_specs=(pl.BlockSpec((nb, tl, D_MODEL), lambda b, l, j: (b, l, 0)),
                   pl.BlockSpec((nb, SUBLANES, fc), nst_map)),
        scratch_shapes=[pltpu.VMEM((nb * tl, D_MODEL), BF16),
                        pltpu.VMEM((nb * tl, D_MODEL), F32),
                        pltpu.VMEM((nf, nb, SUBLANES, fc), F32)],
        compiler_params=_cparams(("parallel", "arbitrary", "arbitrary")),
        name="ffn",
    )(x, state8, norm_pre, w_up, w_up, conv_w, conv_b, w_down, norm_post)


def _ple_kernel(x_ref, p_ref, wp_ref, wg_ref, g_ref, o_ref):
    x = x_ref[...]
    ple = _dot(p_ref[...].astype(BF16), wp_ref[...]) * _sigmoid(_dot(x.astype(BF16), wg_ref[...]))
    o_ref[...] = x + _rms(ple, g_ref[...])


def _ple(x, p, w_ple, w_gate, gain):
    m = x.shape[0]
    tm = min(512, m)
    row = lambda i: (i, 0)
    full = lambda i: (0, 0)
    return pl.pallas_call(
        _ple_kernel,
        out_shape=jax.ShapeDtypeStruct((m, D_MODEL), F32),
        grid=(m // tm,),
        in_specs=[pl.BlockSpec((tm, D_MODEL), row),
                  pl.BlockSpec((tm, PLE_DIM), row),
                  pl.BlockSpec((PLE_DIM, D_MODEL), full),
                  pl.BlockSpec((D_MODEL, D_MODEL), full),
                  pl.BlockSpec((1, D_MODEL), full)],
        out_specs=pl.BlockSpec((tm, D_MODEL), row),
        compiler_params=_cparams(("parallel",)),
        name="ple",
    )(x, p, w_ple, w_gate, gain)


def _pad_rows_to8(state):
    return jnp.pad(state, ((0, 0), (SUBLANES - state.shape[1], 0), (0, 0)))


def _pad_lanes(v):
    return jnp.pad(v, (0, LANES - v.shape[0]))[None, :]


def _constants(cl):
    head = jnp.arange(LANES)[:, None]
    chan = jnp.arange(D_SSM)[None, :] // SSD_HEAD_DIM
    expand = (head == chan).astype(BF16)
    idx = jnp.arange(cl)
    tri_cum = (idx[None, :] <= idx[:, None]).astype(BF16)
    kidx = jnp.arange(SB_BLOCK)
    tri_in = (kidx[:, None] > kidx[None, :]).astype(BF16)
    tri_ext = jnp.concatenate([tri_in, jnp.ones((SB_BLOCK, SB_BLOCK), BF16)], axis=1)
    return expand, tri_cum, tri_ext


def _layer(x, p, conv_st, ssm_st, k_cache, v_cache, ffn_st, lw, layer):
    bsz, length, _ = x.shape
    m = bsz * length
    x2 = x.reshape(m, D_MODEL)
    cl = CHUNK if length >= CHUNK else length
    expand, tri_cum, tri_ext = _constants(cl)

    w_in = lw['w_in']
    o_z, o_xbc, o_dt, o_q, o_g = 0, D_SSM, D_SSM + SSD_CONV_DIM, D_SSM + SSD_CONV_DIM + SSD_HEADS, \
        D_SSM + SSD_CONV_DIM + SSD_HEADS + 3 * SB_WIDTH
    w_main = jnp.concatenate([w_in[:, o_z:o_xbc], w_in[:, o_g:], w_in[:, o_xbc:o_dt], w_in[:, o_q:o_g]],
                             axis=1).astype(BF16)
    w_dt = jnp.pad(w_in[:, o_dt:o_q], ((0, 0), (0, LANES - SSD_HEADS))).astype(BF16)
    g_pre = lw['norm_pre_mix'][None, :]
    zg, xbc, q, k_rows, v_rows, kvb = _in_proj(x2, g_pre, w_main)
    dt_raw = _norm_matmul(x2, g_pre, w_dt)

    xbc3 = xbc.reshape(bsz, length, SSD_CONV_DIM)
    tl = min(256, length)
    y, h_new = _ssd(xbc3, dt_raw.reshape(bsz, length, LANES), _pad_rows_to8(conv_st),
                    lw['ssd_conv_w'], lw['ssd_conv_b'][None, :], _pad_lanes(lw['ssd_dt_bias']),
                    _pad_lanes(lw['ssd_a_log']), jnp.repeat(lw['ssd_d'], SSD_HEAD_DIM)[None, :],
                    ssm_st.reshape(bsz, D_SSM, SSD_STATE), expand, tri_cum, tl=tl, cl=cl)
    new_conv = xbc3[:, length - (SSD_CONV - 1):]
    new_ssm = h_new.reshape(bsz, SSD_HEADS, SSD_HEAD_DIM, SSD_STATE)

    k_new = k_rows.reshape(bsz, length, SB_HEADS, SB_HEAD_DIM)
    v_new = v_rows.reshape(bsz, length, SB_HEADS, SB_HEAD_DIM)
    q3 = q.reshape(bsz, length, SB_WIDTH)
    kvb3 = kvb.reshape(bsz, length, 2 * SB_WIDTH)
    if k_cache is None:
        attn = _sb_prompt(q3, kvb3, tri_ext)
    else:
        depth, _, past = k_cache.shape[:3]
        kvb_pad = jnp.pad(kvb3, ((0, 0), (0, SB_BLOCK - length), (0, 0)))
        attn = _sb_sample(q3, kvb_pad, k_cache.reshape(depth * bsz, past * SB_HEADS, SB_HEAD_DIM),
                          v_cache.reshape(depth * bsz, past * SB_HEADS, SB_HEAD_DIM), tri_ext, layer)

    x2 = _merge(y.reshape(m, D_SSM), zg, attn.reshape(m, SB_WIDTH), x2, lw['ssd_norm'][None, :],
                lw['w_br_ssd'].astype(BF16), lw['w_br_sb'].astype(BF16), lw['w_out'].astype(BF16),
                lw['norm_post_mix'][None, :])

    x3, ffn8 = _ffn(x2.reshape(bsz, length, D_MODEL), _pad_rows_to8(ffn_st), lw['norm_pre_ffn'][None, :],
                    lw['w_up'].astype(BF16), lw['ffn_conv_w'], lw['ffn_conv_b'][None, :],
                    lw['w_down'].astype(BF16), lw['norm_post_ffn'][None, :],
                    nb=max(1, min(bsz, FFN_ROWS // length)), tl=min(FFN_ROWS, length), fc=512)
    new_ffn = ffn8[:, SUBLANES - (FFN_CONV - 1):]

    x2 = _ple(x3.reshape(m, D_MODEL), p.reshape(m, PLE_DIM), lw['w_ple'].astype(BF16),
              lw['w_ple_gate'].astype(BF16), lw['norm_ple'][None, :])
    return x2.reshape(bsz, length, D_MODEL), (new_conv, new_ssm, k_new, v_new, new_ffn)


_LAYER_WEIGHTS = ('norm_pre_mix', 'w_in', 'ssd_conv_w', 'ssd_conv_b', 'ssd_dt_bias', 'ssd_a_log', 'ssd_d',
                  'ssd_norm', 'w_br_ssd', 'w_br_sb', 'w_out', 'norm_post_mix', 'norm_pre_ffn', 'w_up',
                  'ffn_conv_w', 'ffn_conv_b', 'w_down', 'norm_post_ffn', 'w_ple', 'w_ple_gate', 'norm_ple')


def _run(x, p, conv0, ssm0, k0, v0, ffn0, weights):
    per_layer = []
    for i in range(DEPTH):
        lw = {name: weights[name][i] for name in _LAYER_WEIGHTS}
        x, st = _layer(x, p[i], conv0[i], ssm0[i], k0, v0, ffn0[i], lw, i)
        per_layer.append(st)
    return x, [jnp.stack([st[j] for st in per_layer]) for j in range(5)]


def kernel(x_prompt, x_sample, state_ssd_conv, state_ssd, cache_sb_k, cache_sb_v, state_ffn_conv, p_prompt, p_sample, norm_pre_mix, w_in, ssd_conv_w, ssd_conv_b, ssd_dt_bias, ssd_a_log, ssd_d, ssd_norm, w_br_ssd, w_br_sb, w_out, norm_post_mix, norm_pre_ffn, w_up, ffn_conv_w, ffn_conv_b, w_down, norm_post_ffn, w_ple, w_ple_gate, norm_ple):
    weights = dict(norm_pre_mix=norm_pre_mix, w_in=w_in, ssd_conv_w=ssd_conv_w, ssd_conv_b=ssd_conv_b,
                   ssd_dt_bias=ssd_dt_bias, ssd_a_log=ssd_a_log, ssd_d=ssd_d, ssd_norm=ssd_norm,
                   w_br_ssd=w_br_ssd, w_br_sb=w_br_sb, w_out=w_out, norm_post_mix=norm_post_mix,
                   norm_pre_ffn=norm_pre_ffn, w_up=w_up, ffn_conv_w=ffn_conv_w, ffn_conv_b=ffn_conv_b,
                   w_down=w_down, norm_post_ffn=norm_post_ffn, w_ple=w_ple, w_ple_gate=w_ple_gate,
                   norm_ple=norm_ple)
    bp = x_prompt.shape[0]
    zero_conv = jnp.zeros((DEPTH, bp, SSD_CONV - 1, SSD_CONV_DIM), F32)
    zero_ssm = jnp.zeros((DEPTH, bp, SSD_HEADS, SSD_HEAD_DIM, SSD_STATE), F32)
    zero_ffn = jnp.zeros((DEPTH, bp, FFN_CONV - 1, D_FF), F32)
    y_prompt, ps = _run(x_prompt, p_prompt, zero_conv, zero_ssm, None, None, zero_ffn, weights)
    y_sample, ss = _run(x_sample, p_sample, state_ssd_conv, state_ssd, cache_sb_k, cache_sb_v,
                        state_ffn_conv, weights)
    return (y_prompt, y_sample, ps[0], ps[1], ps[2], ps[3], ps[4], ss[0], ss[1], ss[2], ss[3], ss[4])
```

```python
import functools

import jax
import jax.numpy as jnp
from jax import lax
from jax.experimental import pallas as pl
from jax.experimental.pallas import tpu as pltpu

F32 = jnp.float32
BF16 = jnp.bfloat16

D_MODEL = 1024
DEPTH = 2
CHUNK = 64
D_SSM = 2048
SSD_HEAD_DIM = 64
SSD_HEADS = 32
SSD_GROUPS = 4
HEADS_PER_GROUP = 8
SSD_STATE = 128
SSD_CONV = 4
SSD_CONV_DIM = D_SSM + 2 * SSD_GROUPS * SSD_STATE
GROUP_WIDTH = D_SSM // SSD_GROUPS
SB_HEAD_DIM = 128
SB_HEADS = 8
SB_WIDTH = 1024
SB_BLOCK = 128
D_FF = 4096
FFN_CONV = 3
PLE_DIM = 256
EPS = 1e-6
GELU_C = 0.7978845608028654
GELU_A = 0.044715
SB_SCALE = SB_HEAD_DIM ** -0.5
SB_HEAD_GROUP = 8
SB_DEAD_LOG = 120.0

LANES = 128
SUBLANES = 8
VMEM_LIMIT = 56 * 1024 * 1024
FFN_ROWS = 1024


def _cparams(sem):
    return pltpu.CompilerParams(dimension_semantics=sem, vmem_limit_bytes=VMEM_LIMIT)


def _rms(x, gain):
    return x * lax.rsqrt(jnp.mean(x * x, axis=-1, keepdims=True) + EPS) * gain


def _sigmoid(x):
    return 0.5 + 0.5 * jnp.tanh(0.5 * x)


def _silu(x):
    h = 0.5 * x
    return h + h * jnp.tanh(h)


def _dot(a, b):
    return jnp.dot(a, b, preferred_element_type=F32)


def _dot_nt(a, b):
    return lax.dot_general(a, b, (((1,), (1,)), ((), ())), preferred_element_type=F32)


def _dot_tn(a, b):
    return lax.dot_general(a, b, (((0,), (0,)), ((), ())), preferred_element_type=F32)


def _split_bf16(v, n):
    parts = []
    r = v
    for _ in range(n):
        p = r.astype(BF16)
        parts.append(p)
        r = r - p.astype(F32)
    return parts


def _shifted_rows(x, hist, k):
    nseq = hist.shape[0]
    tl = x.shape[0] // nseq
    rolled = pltpu.roll(x, k, 0)
    row = lax.broadcasted_iota(jnp.int32, (SUBLANES, x.shape[1]), 0)
    pieces = []
    for b in range(nseq):
        start = b * tl
        pieces.append(jnp.where(row < k, pltpu.roll(hist[b], k, 0), rolled[start:start + SUBLANES]))
        if tl > SUBLANES:
            pieces.append(rolled[start + SUBLANES:start + tl])
    return pieces[0] if len(pieces) == 1 else jnp.concatenate(pieces, axis=0)


def _causal_conv(x, hist, w_ref, b_ref, width):
    acc = b_ref[...]
    for tap in range(width):
        k = width - 1 - tap
        sh = x if k == 0 else _shifted_rows(x, hist, k)
        acc = acc + sh * w_ref[tap:tap + 1, :]
    return acc


def _norm_matmul_kernel(x_ref, g_ref, w_ref, o_ref, h_ref, *, normalize):
    @pl.when(pl.program_id(1) == 0)
    def _():
        x = x_ref[...]
        if normalize:
            x = _rms(x, g_ref[...])
        h_ref[...] = x.astype(BF16)

    o_ref[...] = _dot(h_ref[...], w_ref[...])


def _norm_matmul(x, gain, w, *, normalize=True):
    m, k = x.shape
    n = w.shape[1]
    tm = min(1024, m)
    tn = min(1024, n)
    return pl.pallas_call(
        functools.partial(_norm_matmul_kernel, normalize=normalize),
        out_shape=jax.ShapeDtypeStruct((m, n), F32),
        grid=(m // tm, n // tn),
        in_specs=[pl.BlockSpec((tm, k), lambda i, j: (i, 0)),
                  pl.BlockSpec((1, k), lambda i, j: (0, 0)),
                  pl.BlockSpec((k, tn), lambda i, j: (0, j))],
        out_specs=pl.BlockSpec((tm, tn), lambda i, j: (i, j)),
        scratch_shapes=[pltpu.VMEM((tm, k), BF16)],
        compiler_params=_cparams(("parallel", "arbitrary")),
        name="norm_matmul",
    )(x, gain, w)


PROJ_TN = 512
_PROJ_WIDTHS = (2 * D_SSM, SSD_CONV_DIM, SB_WIDTH, SB_WIDTH, SB_WIDTH)
_PROJ_FIRST = tuple(sum(_PROJ_WIDTHS[:n]) // PROJ_TN for n in range(len(_PROJ_WIDTHS) + 1))


def _in_proj_kernel(x_ref, g_ref, w_ref, zg_ref, xbc_ref, q_ref, k_ref, v_ref, kvb_ref, h_ref):
    j = pl.program_id(1)

    @pl.when(j == 0)
    def _():
        h_ref[...] = _rms(x_ref[...], g_ref[...]).astype(BF16)

    def owns(n):
        return jnp.logical_and(j >= _PROJ_FIRST[n], j < _PROJ_FIRST[n + 1])

    def proj():
        return _dot(h_ref[...], w_ref[...])

    @pl.when(owns(0))
    def _():
        zg_ref[...] = proj().astype(BF16)

    @pl.when(owns(1))
    def _():
        xbc_ref[...] = proj().astype(BF16)

    @pl.when(owns(2))
    def _():
        q_ref[...] = proj().astype(BF16)

    heads_per_tile = PROJ_TN // SB_HEAD_DIM
    tm = x_ref.shape[0]
    for n, rows_ref in ((3, k_ref), (4, v_ref)):
        for t in range(_PROJ_FIRST[n + 1] - _PROJ_FIRST[n]):
            @pl.when(j == _PROJ_FIRST[n] + t)
            def _(rows_ref=rows_ref, t=t):
                r = proj()
                kvb_ref[...] = r.astype(BF16)
                for hh in range(heads_per_tile):
                    head = t * heads_per_tile + hh
                    rows_ref[pl.ds(head, tm, stride=SB_HEADS), :] = r[:, hh * SB_HEAD_DIM:(hh + 1) * SB_HEAD_DIM]


def _in_proj(x, gain, w):
    m, k = x.shape
    tm = min(1024, m)
    tn = PROJ_TN
    bf16 = lambda n: jax.ShapeDtypeStruct((m, n), BF16)
    head_rows = jax.ShapeDtypeStruct((m * SB_HEADS, SB_HEAD_DIM), F32)

    def tile_of(n_first, n_last):
        first, count = _PROJ_FIRST[n_first], _PROJ_FIRST[n_last + 1] - _PROJ_FIRST[n_first]
        return pl.BlockSpec((tm, tn), lambda i, j: (i, jnp.clip(j - first, 0, count - 1)))

    rows_spec = pl.BlockSpec((tm * SB_HEADS, SB_HEAD_DIM), lambda i, j: (i, 0))
    return pl.pallas_call(
        _in_proj_kernel,
        out_shape=(bf16(2 * D_SSM), bf16(SSD_CONV_DIM), bf16(SB_WIDTH), head_rows, head_rows, bf16(2 * SB_WIDTH)),
        grid=(m // tm, _PROJ_FIRST[-1]),
        in_specs=[pl.BlockSpec((tm, k), lambda i, j: (i, 0)),
                  pl.BlockSpec((1, k), lambda i, j: (0, 0)),
                  pl.BlockSpec((k, tn), lambda i, j: (0, j))],
        out_specs=(tile_of(0, 0), tile_of(1, 1), tile_of(2, 2), rows_spec, rows_spec, tile_of(3, 4)),
        scratch_shapes=[pltpu.VMEM((tm, k), BF16)],
        compiler_params=_cparams(("parallel", "arbitrary")),
        name="in_proj",
    )(x, gain, w)


def _ssd_kernel(xbc_ref, dt_ref, cst_ref, cw_ref, cb_ref, dtb_ref, alog_ref, dexp_ref, h0_ref, e_ref, tri_ref,
                y_ref, hout_ref, h_scr, carry_scr, xc_scr, dt_scr, *, cl, nchunk):
    l = pl.program_id(1)

    @pl.when(l == 0)
    def _():
        h_scr[...] = h0_ref[0]
        carry_scr[...] = cst_ref[0]

    xt = xbc_ref[0].astype(F32)
    tl = xt.shape[0]
    conv = _causal_conv(xt, carry_scr[...][None], cw_ref, cb_ref, SSD_CONV)
    carry_scr[...] = xt[tl - SUBLANES:tl]
    xc_scr[...] = _silu(conv)
    dtr = dt_ref[0] + dtb_ref[...]
    dt_scr[...] = jnp.maximum(dtr, 0.0) + jnp.log1p(jnp.exp(-jnp.abs(dtr)))

    a_head = -jnp.exp(alog_ref[...])
    tri = tri_ref[...]
    expand = e_ref[...]
    rowi = lax.broadcasted_iota(jnp.int32, (cl, cl), 0)
    coli = lax.broadcasted_iota(jnp.int32, (cl, cl), 1)
    causal = coli <= rowi
    lane = lax.broadcasted_iota(jnp.int32, (1, LANES), 1)
    half_masks = ((lane < SSD_HEAD_DIM).astype(F32), (lane >= SSD_HEAD_DIM).astype(F32))

    def expand_heads(v):
        hi, lo = _split_bf16(v, 2)
        return _dot(hi, expand) + _dot(lo, expand)

    def chunk(r0):
        rows = pl.ds(r0, cl)
        xs = xc_scr[rows, 0:D_SSM]
        bm = xc_scr[rows, D_SSM:D_SSM + SSD_GROUPS * SSD_STATE]
        cm = xc_scr[rows, D_SSM + SSD_GROUPS * SSD_STATE:SSD_CONV_DIM]
        dtc = dt_scr[rows, :]
        a = dtc * a_head
        acum = sum(_dot(tri, p) for p in _split_bf16(a, 3))
        if cl < LANES:
            acum_sq = jnp.concatenate([acum, jnp.zeros((LANES - cl, LANES), F32)], axis=0)
        else:
            acum_sq = acum
        acum_t = acum_sq.T
        a_end = acum[cl - 1:cl, :]
        dt_e = expand_heads(dtc)
        dec_end_e = expand_heads(jnp.exp(a_end - acum))
        dec_in_e = expand_heads(jnp.exp(acum))
        chunk_decay = jnp.broadcast_to(jnp.exp(acum_t[:, cl - 1:cl]), (LANES, LANES))
        x_dt = xs * dt_e
        x_end = x_dt * dec_end_e
        for g in range(SSD_GROUPS):
            gs = slice(g * SSD_STATE, (g + 1) * SSD_STATE)
            gw = slice(g * GROUP_WIDTH, (g + 1) * GROUP_WIDTH)
            bg = bm[:, gs].astype(BF16)
            cg = cm[:, gs].astype(BF16)
            cb = _dot_nt(cg, bg)
            h_prev = h_scr[gw, :]
            y_off = _dot_nt(cg, h_prev.astype(BF16)) * dec_in_e[:, gw]
            st = _dot_tn(x_end[:, gw].astype(BF16), bg)
            for pr in range(HEADS_PER_GROUP // 2):
                ps = slice(g * GROUP_WIDTH + pr * LANES, g * GROUP_WIDTH + (pr + 1) * LANES)
                xp = x_dt[:, ps]
                yp = y_off[:, pr * LANES:(pr + 1) * LANES]
                for sub in range(2):
                    e = g * HEADS_PER_GROUP + pr * 2 + sub
                    seg = acum[:, e:e + 1] - acum_t[e:e + 1, 0:cl]
                    dec = jnp.exp(jnp.where(causal, seg, -jnp.inf))
                    m_e = (cb * dec).astype(BF16)
                    yp = yp + _dot(m_e, (xp * half_masks[sub]).astype(BF16))
                    hs = slice(e * SSD_HEAD_DIM, (e + 1) * SSD_HEAD_DIM)
                    ls = slice((pr * 2 + sub) * SSD_HEAD_DIM, (pr * 2 + sub + 1) * SSD_HEAD_DIM)
                    h_scr[hs, :] = h_scr[hs, :] * chunk_decay[e:e + 1, :] + st[ls, :]
                y_ref[0, rows, ps] = yp + dexp_ref[:, ps] * xs[:, ps]

    if nchunk == 1:
        chunk(0)
    else:
        def body(c, carry):
            chunk(pl.multiple_of(c * cl, cl))
            return carry
        lax.fori_loop(0, nchunk, body, 0)

    @pl.when(l == pl.num_programs(1) - 1)
    def _():
        hout_ref[0] = h_scr[...]


def _ssd(xbc, dt_raw, conv_state8, conv_w, conv_b, dt_bias, a_log, d_exp, h0, expand, tri, *, tl, cl):
    bsz, length, _ = xbc.shape
    kernel = functools.partial(_ssd_kernel, cl=cl, nchunk=tl // cl)
    full2 = lambda b, l: (0, 0)
    return pl.pallas_call(
        kernel,
        out_shape=(jax.ShapeDtypeStruct((bsz, length, D_SSM), F32),
                   jax.ShapeDtypeStruct((bsz, D_SSM, SSD_STATE), F32)),
        grid=(bsz, length // tl),
        in_specs=[pl.BlockSpec((1, tl, SSD_CONV_DIM), lambda b, l: (b, l, 0)),
                  pl.BlockSpec((1, tl, LANES), lambda b, l: (b, l, 0)),
                  pl.BlockSpec((1, SUBLANES, SSD_CONV_DIM), lambda b, l: (b, 0, 0)),
                  pl.BlockSpec((SSD_CONV, SSD_CONV_DIM), full2),
                  pl.BlockSpec((1, SSD_CONV_DIM), full2),
                  pl.BlockSpec((1, LANES), full2),
                  pl.BlockSpec((1, LANES), full2),
                  pl.BlockSpec((1, D_SSM), full2),
                  pl.BlockSpec((1, D_SSM, SSD_STATE), lambda b, l: (b, 0, 0)),
                  pl.BlockSpec((LANES, D_SSM), full2),
                  pl.BlockSpec((cl, cl), full2)],
        out_specs=(pl.BlockSpec((1, tl, D_SSM), lambda b, l: (b, l, 0)),
                   pl.BlockSpec((1, D_SSM, SSD_STATE), lambda b, l: (b, 0, 0))),
        scratch_shapes=[pltpu.VMEM((D_SSM, SSD_STATE), F32),
                        pltpu.VMEM((SUBLANES, SSD_CONV_DIM), F32),
                        pltpu.VMEM((tl, SSD_CONV_DIM), F32),
                        pltpu.VMEM((tl, LANES), F32)],
        compiler_params=_cparams(("parallel", "arbitrary")),
        name="ssd",
    )(xbc, dt_raw, conv_state8, conv_w, conv_b, dt_bias, a_log, d_exp, h0, expand, tri)


def _sb_blocks(qs, kbs, vbs, laters, accs, tri_ext, earlier):
    zs = [_dot_nt(q, kb) * SB_SCALE for q, kb in zip(qs, kbs)]
    log_keeps, log_betas = [], []
    for z in zs:
        l1p = jnp.log(1.0 + jnp.exp(-jnp.abs(z)))
        log_keep = -(jnp.maximum(z, 0.0) + l1p)
        if earlier is not None:
            log_keep = jnp.where(earlier, log_keep, 0.0)
        log_keeps.append(log_keep)
        log_betas.append(jnp.minimum(z, 0.0) - l1p)
    exts = [_dot(lk.astype(BF16), tri_ext) for lk in log_keeps]
    ws = []
    for log_beta, ext, later in zip(log_betas, exts, laters):
        w = jnp.exp(log_beta + ext[:, :SB_BLOCK] + later)
        if earlier is not None:
            w = jnp.where(earlier, w, 0.0)
        ws.append(w.astype(BF16))
    accs = [acc + _dot(w, vb) for acc, w, vb in zip(accs, ws, vbs)]
    laters = [later + ext[:, SB_BLOCK:] for later, ext in zip(laters, exts)]
    return laters, accs


def _sb_cols(h):
    return slice(h * SB_HEAD_DIM, (h + 1) * SB_HEAD_DIM)


def _sb_alive(later_scr):
    return jnp.max(later_scr[...]) > -SB_DEAD_LOG


def _sb_heads(q_ref, blocks, later_scr, acc_scr, tri_ext, init):
    tq = q_ref.shape[1]
    for h0 in range(0, SB_HEADS, SB_HEAD_GROUP):
        heads = range(h0, h0 + SB_HEAD_GROUP)
        qs = [q_ref[0, :, _sb_cols(h)] for h in heads]
        if init:
            laters = [jnp.zeros((tq, SB_BLOCK), F32) for _ in heads]
            accs = [jnp.zeros((tq, SB_HEAD_DIM), F32) for _ in heads]
        else:
            laters = [later_scr[h] for h in heads]
            accs = [acc_scr[h] for h in heads]
        for k_of, v_of, earlier in blocks:
            laters, accs = _sb_blocks(qs, [k_of(h) for h in heads], [v_of(h) for h in heads],
                                      laters, accs, tri_ext, earlier)
        for h, later, acc in zip(heads, laters, accs):
            later_scr[h] = later
            acc_scr[h] = acc


def _sb_kv_of(read):
    return (lambda h: read(_sb_cols(h))), (lambda h: read(_sb_cols(SB_HEADS + h)))


def _sb_diag_mask(tq):
    key = lax.broadcasted_iota(jnp.int32, (tq, SB_BLOCK), 1)
    query = lax.broadcasted_iota(jnp.int32, (tq, SB_BLOCK), 0)
    return key < query


def _sb_prompt_kernel(q_ref, kv0_ref, kv1_ref, kv2_ref, kv_hbm, tri_ref, o_ref,
                      later_scr, acc_scr, kv_buf, sem):
    b = pl.program_id(0)
    qi = pl.program_id(1)
    tri_ext = tri_ref[...]
    state = (later_scr, acc_scr, tri_ext)

    k0, v0 = _sb_kv_of(lambda c: kv0_ref[0, :, c])
    _sb_heads(q_ref, [(k0, v0, _sb_diag_mask(q_ref.shape[1]))], *state, init=True)

    @pl.when(qi >= 2)
    def _():
        k1, v1 = _sb_kv_of(lambda c: kv1_ref[0, :, c])
        k2, v2 = _sb_kv_of(lambda c: kv2_ref[0, :, c])
        _sb_heads(q_ref, [(k1, v1, None), (k2, v2, None)], *state, init=False)

    def more(carry):
        j, alive = carry
        return jnp.logical_and(j >= 0, alive)

    def body(carry):
        j, _ = carry
        rows = pl.ds(pl.multiple_of(j * SB_BLOCK, SB_BLOCK), SB_BLOCK)
        copy = pltpu.make_async_copy(kv_hbm.at[b, rows, :], kv_buf, sem)
        copy.start()
        copy.wait()
        kb, vb = _sb_kv_of(lambda c: kv_buf[:, c])
        _sb_heads(q_ref, [(kb, vb, None)], *state, init=False)
        return j - 1, _sb_alive(later_scr)

    lax.while_loop(more, body, (jnp.where(qi >= 2, qi - 3, qi - 1), _sb_alive(later_scr)))
    for h in range(SB_HEADS):
        o_ref[0, :, _sb_cols(h)] = acc_scr[h].astype(o_ref.dtype)


def _sb_prompt(q, kvb, tri_ext):
    bsz, length, _ = q.shape
    tq = SB_BLOCK
    blk = lambda back: pl.BlockSpec((1, SB_BLOCK, 2 * SB_WIDTH), lambda b, i: (b, jnp.maximum(i - back, 0), 0))
    return pl.pallas_call(
        _sb_prompt_kernel,
        out_shape=jax.ShapeDtypeStruct((bsz, length, SB_WIDTH), BF16),
        grid=(bsz, length // tq),
        in_specs=[pl.BlockSpec((1, tq, SB_WIDTH), lambda b, i: (b, i, 0)),
                  blk(0), blk(1), blk(2),
                  pl.BlockSpec(memory_space=pl.ANY),
                  pl.BlockSpec((SB_BLOCK, 2 * SB_BLOCK), lambda b, i: (0, 0))],
        out_specs=pl.BlockSpec((1, tq, SB_WIDTH), lambda b, i: (b, i, 0)),
        scratch_shapes=[pltpu.VMEM((SB_HEADS, tq, SB_BLOCK), F32),
                        pltpu.VMEM((SB_HEADS, tq, SB_HEAD_DIM), F32),
                        pltpu.VMEM((SB_BLOCK, 2 * SB_WIDTH), BF16),
                        pltpu.SemaphoreType.DMA(())],
        compiler_params=_cparams(("parallel", "arbitrary")),
        name="sb_prompt",
    )(q, kvb, kvb, kvb, kvb, tri_ext)


SB_BLOCK_ROWS = SB_BLOCK * SB_HEADS
SB_RECENT_BLOCKS = 2


def _sb_cache_heads(read):
    return lambda first_row: (lambda h: read(pl.ds(first_row + h, SB_BLOCK, stride=SB_HEADS)).astype(BF16))


def _sb_sample_kernel(q_ref, kvn_ref, kr_ref, vr_ref, k_hbm, v_hbm, tri_ref, o_ref,
                      later_scr, acc_scr, k_buf, v_buf, sem, *, npast, first):
    b = pl.program_id(0)
    tri_ext = tri_ref[...]
    state = (later_scr, acc_scr, tri_ext)
    kn, vn = _sb_kv_of(lambda c: kvn_ref[0, :, c])
    _sb_heads(q_ref, [(kn, vn, _sb_diag_mask(q_ref.shape[1]))], *state, init=True)

    k_recent = _sb_cache_heads(lambda rows: kr_ref[0, rows, :])
    v_recent = _sb_cache_heads(lambda rows: vr_ref[0, rows, :])
    recent = [(k_recent(t * SB_BLOCK_ROWS), v_recent(t * SB_BLOCK_ROWS), None)
              for t in reversed(range(SB_RECENT_BLOCKS))]
    _sb_heads(q_ref, recent, *state, init=False)

    def more(carry):
        j, alive = carry
        return jnp.logical_and(j >= 0, alive)

    def body(carry):
        j, _ = carry
        rows = pl.ds(pl.multiple_of(j * SB_BLOCK_ROWS, SB_BLOCK_ROWS), SB_BLOCK_ROWS)
        copies = [pltpu.make_async_copy(hbm.at[first + b, rows, :], buf, sem.at[n])
                  for n, (hbm, buf) in enumerate(((k_hbm, k_buf), (v_hbm, v_buf)))]
        for copy in copies:
            copy.start()
        for copy in copies:
            copy.wait()
        kb = _sb_cache_heads(lambda r: k_buf[r, :])(0)
        vb = _sb_cache_heads(lambda r: v_buf[r, :])(0)
        _sb_heads(q_ref, [(kb, vb, None)], *state, init=False)
        return j - 1, _sb_alive(later_scr)

    lax.while_loop(more, body, (npast - SB_RECENT_BLOCKS - 1, _sb_alive(later_scr)))
    for h in range(SB_HEADS):
        o_ref[0, :, _sb_cols(h)] = acc_scr[h].astype(o_ref.dtype)


def _sb_sample(q, kvb_new_pad, k_past, v_past, tri_ext, layer):
    bsz, lq, _ = q.shape
    rows = k_past.shape[1]
    npast = rows // SB_BLOCK_ROWS
    assert rows % SB_BLOCK_ROWS == 0 and npast >= SB_RECENT_BLOCKS and lq <= SB_BLOCK
    first = layer * bsz
    recent_rows = SB_RECENT_BLOCKS * SB_BLOCK_ROWS
    assert rows % recent_rows == 0
    recent = pl.BlockSpec((1, recent_rows, SB_HEAD_DIM), lambda b: (first + b, rows // recent_rows - 1, 0))
    return pl.pallas_call(
        functools.partial(_sb_sample_kernel, npast=npast, first=first),
        out_shape=jax.ShapeDtypeStruct((bsz, lq, SB_WIDTH), BF16),
        grid=(bsz,),
        in_specs=[pl.BlockSpec((1, lq, SB_WIDTH), lambda b: (b, 0, 0)),
                  pl.BlockSpec((1, SB_BLOCK, 2 * SB_WIDTH), lambda b: (b, 0, 0)),
                  recent, recent,
                  pl.BlockSpec(memory_space=pl.ANY),
                  pl.BlockSpec(memory_space=pl.ANY),
                  pl.BlockSpec((SB_BLOCK, 2 * SB_BLOCK), lambda b: (0, 0))],
        out_specs=pl.BlockSpec((1, lq, SB_WIDTH), lambda b: (b, 0, 0)),
        scratch_shapes=[pltpu.VMEM((SB_HEADS, lq, SB_BLOCK), F32),
                        pltpu.VMEM((SB_HEADS, lq, SB_HEAD_DIM), F32),
                        pltpu.VMEM((SB_BLOCK_ROWS, SB_HEAD_DIM), F32),
                        pltpu.VMEM((SB_BLOCK_ROWS, SB_HEAD_DIM), F32),
                        pltpu.SemaphoreType.DMA((2,))],
        compiler_params=_cparams(("parallel",)),
        name="sb_sample",
    )(q, kvb_new_pad, k_past, v_past, k_past, v_past, tri_ext)


def _merge_kernel(y_ref, z_ref, gate_ref, attn_ref, x_ref, nw_ref, w1_ref, w2_ref, w3_ref, gpost_ref, o_ref):
    z = z_ref[...].astype(F32)
    g = y_ref[...] * _silu(z)
    b1 = None
    for k in range(SSD_GROUPS):
        gw = slice(k * GROUP_WIDTH, (k + 1) * GROUP_WIDTH)
        gk = _rms(g[:, gw], nw_ref[:, gw])
        part = _dot(gk.astype(BF16), w1_ref[gw, :])
        b1 = part if b1 is None else b1 + part
    b2 = _dot(attn_ref[...], w2_ref[...])
    gl = gate_ref[...].astype(F32)
    m = _sigmoid(gl[:, :D_MODEL]) * b1 + _sigmoid(gl[:, D_MODEL:]) * b2
    mixed = _dot(m.astype(BF16), w3_ref[...])
    o_ref[...] = x_ref[...] + _rms(mixed, gpost_ref[...])


def _merge(y, zg, attn, x, ssd_norm, w_br_ssd, w_br_sb, w_out, norm_post):
    m = x.shape[0]
    tm = min(256, m)
    row = lambda i: (i, 0)
    full = lambda i: (0, 0)
    return pl.pallas_call(
        _merge_kernel,
        out_shape=jax.ShapeDtypeStruct((m, D_MODEL), F32),
        grid=(m // tm,),
        in_specs=[pl.BlockSpec((tm, D_SSM), row),
                  pl.BlockSpec((tm, D_SSM), row),
                  pl.BlockSpec((tm, 2 * D_MODEL), lambda i: (i, 1)),
                  pl.BlockSpec((tm, SB_WIDTH), row),
                  pl.BlockSpec((tm, D_MODEL), row),
                  pl.BlockSpec((1, D_SSM), full),
                  pl.BlockSpec((D_SSM, D_MODEL), full),
                  pl.BlockSpec((SB_WIDTH, D_MODEL), full),
                  pl.BlockSpec((D_MODEL, D_MODEL), full),
                  pl.BlockSpec((1, D_MODEL), full)],
        out_specs=pl.BlockSpec((tm, D_MODEL), row),
        compiler_params=_cparams(("parallel",)),
        name="merge",
    )(y, zg, zg, attn, x, ssd_norm, w_br_ssd, w_br_sb, w_out, norm_post)


def _ffn_kernel(x_ref, st_ref, gpre_ref, wg_ref, wu_ref, cw_ref, cb_ref, wd_ref, gpost_ref,
                o_ref, nst_ref, h_scr, acc_scr, carry_scr):
    l = pl.program_id(1)
    j = pl.program_id(2)
    nb, tl, _ = x_ref.shape

    @pl.when(j == 0)
    def _():
        h_scr[...] = _rms(x_ref[...].reshape(nb * tl, D_MODEL), gpre_ref[...]).astype(BF16)
        acc_scr[...] = jnp.zeros_like(acc_scr)

    @pl.when(l == 0)
    def _():
        carry_scr[j] = st_ref[...]

    h = h_scr[...]
    gate = _dot(h, wg_ref[...])
    up = _dot(h, wu_ref[...])
    conv = _causal_conv(gate, carry_scr[j], cw_ref, cb_ref, FFN_CONV)
    tail = gate.reshape(nb, tl, gate.shape[1])[:, tl - SUBLANES:, :]
    carry_scr[j] = tail

    @pl.when(l == pl.num_programs(1) - 1)
    def _():
        nst_ref[...] = tail

    half = 0.5 * conv
    gelu = half + half * jnp.tanh(conv * (GELU_C + (GELU_C * GELU_A) * (conv * conv)))
    acc_scr[...] += _dot((gelu * up).astype(BF16), wd_ref[...])

    @pl.when(j == pl.num_programs(2) - 1)
    def _():
        out = x_ref[...].reshape(nb * tl, D_MODEL) + _rms(acc_scr[...], gpost_ref[...])
        o_ref[...] = out.reshape(nb, tl, D_MODEL)


def _ffn(x, state8, norm_pre, w_up, conv_w, conv_b, w_down, norm_post, *, nb, tl, fc):
    bsz, length, _ = x.shape
    nf = D_FF // fc
    nl = length // tl
    full = lambda b, l, j: (0, 0)
    nst_map = lambda b, l, j: (b, 0, jnp.where(l == nl - 1, j, 0))
    return pl.pallas_call(
        _ffn_kernel,
        out_shape=(jax.ShapeDtypeStruct((bsz, length, D_MODEL), F32),
                   jax.ShapeDtypeStruct((bsz, SUBLANES, D_FF), F32)),
        grid=(bsz // nb, nl, nf),
        in_specs=[pl.BlockSpec((nb, tl, D_MODEL), lambda b, l, j: (b, l, 0)),
                  pl.BlockSpec((nb, SUBLANES, fc), lambda b, l, j: (b, 0, j)),
                  pl.BlockSpec((1, D_MODEL), full),
                  pl.BlockSpec((D_MODEL, fc), lambda b, l, j: (0, j)),
                  pl.BlockSpec((D_MODEL, fc), lambda b, l, j: (0, nf + j)),
                  pl.BlockSpec((FFN_CONV, fc), lambda b, l, j: (0, j)),
                  pl.BlockSpec((1, fc), lambda b, l, j: (0, j)),
                  pl.BlockSpec((fc, D_MODEL), lambda b, l, j: (j, 0)),
                  pl.BlockSpec((1, D_MODEL), full)],
        out_specs=(pl.BlockSpec((nb, tl, D_MODEL), lambda b, l, j: (b, l, 0)),
                   pl.BlockSpec((nb, SUBLANES, fc), nst_map)),
        scratch_shapes=[pltpu.VMEM((nb * tl, D_MODEL), BF16),
                        pltpu.VMEM((nb * tl, D_MODEL), F32),
                        pltpu.VMEM((nf, nb, SUBLANES, fc), F32)],
        compiler_params=_cparams(("parallel", "arbitrary", "arbitrary")),
        name="ffn",
    )(x, state8, norm_pre, w_up, w_up, conv_w, conv_b, w_down, norm_post)


def _ple_kernel(x_ref, p_ref, wp_ref, wg_ref, g_ref, o_ref):
    x = x_ref[...]
    ple = _dot(p_ref[...].astype(BF16), wp_ref[...]) * _sigmoid(_dot(x.astype(BF16), wg_ref[...]))
    o_ref[...] = x + _rms(ple, g_ref[...])


def _ple(x, p, w_ple, w_gate, gain):
    m = x.shape[0]
    tm = min(512, m)
    row = lambda i: (i, 0)
    full = lambda i: (0, 0)
    return pl.pallas_call(
        _ple_kernel,
        out_shape=jax.ShapeDtypeStruct((m, D_MODEL), F32),
        grid=(m // tm,),
        in_specs=[pl.BlockSpec((tm, D_MODEL), row),
                  pl.BlockSpec((tm, PLE_DIM), row),
                  pl.BlockSpec((PLE_DIM, D_MODEL), full),
                  pl.BlockSpec((D_MODEL, D_MODEL), full),
                  pl.BlockSpec((1, D_MODEL), full)],
        out_specs=pl.BlockSpec((tm, D_MODEL), row),
        compiler_params=_cparams(("parallel",)),
        name="ple",
    )(x, p, w_ple, w_gate, gain)


def _pad_rows_to8(state):
    return jnp.pad(state, ((0, 0), (SUBLANES - state.shape[1], 0), (0, 0)))


def _pad_lanes(v):
    return jnp.pad(v, (0, LANES - v.shape[0]))[None, :]


def _constants(cl):
    head = jnp.arange(LANES)[:, None]
    chan = jnp.arange(D_SSM)[None, :] // SSD_HEAD_DIM
    expand = (head == chan).astype(BF16)
    idx = jnp.arange(cl)
    tri_cum = (idx[None, :] <= idx[:, None]).astype(BF16)
    kidx = jnp.arange(SB_BLOCK)
    tri_in = (kidx[:, None] > kidx[None, :]).astype(BF16)
    tri_ext = jnp.concatenate([tri_in, jnp.ones((SB_BLOCK, SB_BLOCK), BF16)], axis=1)
    return expand, tri_cum, tri_ext


def _layer(x, p, conv_st, ssm_st, k_cache, v_cache, ffn_st, lw, layer):
    bsz, length, _ = x.shape
    m = bsz * length
    x2 = x.reshape(m, D_MODEL)
    cl = CHUNK if length >= CHUNK else length
    expand, tri_cum, tri_ext = _constants(cl)

    w_in = lw['w_in']
    o_z, o_xbc, o_dt, o_q, o_g = 0, D_SSM, D_SSM + SSD_CONV_DIM, D_SSM + SSD_CONV_DIM + SSD_HEADS, \
        D_SSM + SSD_CONV_DIM + SSD_HEADS + 3 * SB_WIDTH
    w_main = jnp.concatenate([w_in[:, o_z:o_xbc], w_in[:, o_g:], w_in[:, o_xbc:o_dt], w_in[:, o_q:o_g]],
                             axis=1).astype(BF16)
    w_dt = jnp.pad(w_in[:, o_dt:o_q], ((0, 0), (0, LANES - SSD_HEADS))).astype(BF16)
    g_pre = lw['norm_pre_mix'][None, :]
    zg, xbc, q, k_rows, v_rows, kvb = _in_proj(x2, g_pre, w_main)
    dt_raw = _norm_matmul(x2, g_pre, w_dt)

    xbc3 = xbc.reshape(bsz, length, SSD_CONV_DIM)
    tl = min(256, length)
    y, h_new = _ssd(xbc3, dt_raw.reshape(bsz, length, LANES), _pad_rows_to8(conv_st),
                    lw['ssd_conv_w'], lw['ssd_conv_b'][None, :], _pad_lanes(lw['ssd_dt_bias']),
                    _pad_lanes(lw['ssd_a_log']), jnp.repeat(lw['ssd_d'], SSD_HEAD_DIM)[None, :],
                    ssm_st.reshape(bsz, D_SSM, SSD_STATE), expand, tri_cum, tl=tl, cl=cl)
    new_conv = xbc3[:, length - (SSD_CONV - 1):].astype(F32)
    new_ssm = h_new.reshape(bsz, SSD_HEADS, SSD_HEAD_DIM, SSD_STATE)

    k_new = k_rows.reshape(bsz, length, SB_HEADS, SB_HEAD_DIM)
    v_new = v_rows.reshape(bsz, length, SB_HEADS, SB_HEAD_DIM)
    q3 = q.reshape(bsz, length, SB_WIDTH)
    kvb3 = kvb.reshape(bsz, length, 2 * SB_WIDTH)
    if k_cache is None:
        attn = _sb_prompt(q3, kvb3, tri_ext)
    else:
        depth, _, past = k_cache.shape[:3]
        kvb_pad = jnp.pad(kvb3, ((0, 0), (0, SB_BLOCK - length), (0, 0)))
        attn = _sb_sample(q3, kvb_pad, k_cache.reshape(depth * bsz, past * SB_HEADS, SB_HEAD_DIM),
                          v_cache.reshape(depth * bsz, past * SB_HEADS, SB_HEAD_DIM), tri_ext, layer)

    x2 = _merge(y.reshape(m, D_SSM), zg, attn.reshape(m, SB_WIDTH), x2, lw['ssd_norm'][None, :],
                lw['w_br_ssd'].astype(BF16), lw['w_br_sb'].astype(BF16), lw['w_out'].astype(BF16),
                lw['norm_post_mix'][None, :])

    x3, ffn8 = _ffn(x2.reshape(bsz, length, D_MODEL), _pad_rows_to8(ffn_st), lw['norm_pre_ffn'][None, :],
                    lw['w_up'].astype(BF16), lw['ffn_conv_w'], lw['ffn_conv_b'][None, :],
                    lw['w_down'].astype(BF16), lw['norm_post_ffn'][None, :],
                    nb=max(1, min(bsz, FFN_ROWS // length)), tl=min(FFN_ROWS, length), fc=512)
    new_ffn = ffn8[:, SUBLANES - (FFN_CONV - 1):]

    x2 = _ple(x3.reshape(m, D_MODEL), p.reshape(m, PLE_DIM), lw['w_ple'].astype(BF16),
              lw['w_ple_gate'].astype(BF16), lw['norm_ple'][None, :])
    return x2.reshape(bsz, length, D_MODEL), (new_conv, new_ssm, k_new, v_new, new_ffn)


_LAYER_WEIGHTS = ('norm_pre_mix', 'w_in', 'ssd_conv_w', 'ssd_conv_b', 'ssd_dt_bias', 'ssd_a_log', 'ssd_d',
                  'ssd_norm', 'w_br_ssd', 'w_br_sb', 'w_out', 'norm_post_mix', 'norm_pre_ffn', 'w_up',
                  'ffn_conv_w', 'ffn_conv_b', 'w_down', 'norm_post_ffn', 'w_ple', 'w_ple_gate', 'norm_ple')


def _run(x, p, conv0, ssm0, k0, v0, ffn0, weights):
    per_layer = []
    for i in range(DEPTH):
        lw = {name: weights[name][i] for name in _LAYER_WEIGHTS}
        x, st = _layer(x, p[i], conv0[i], ssm0[i], k0, v0, ffn0[i], lw, i)
        per_layer.append(st)
    return x, [jnp.stack([st[j] for st in per_layer]) for j in range(5)]


def kernel(x_prompt, x_sample, state_ssd_conv, state_ssd, cache_sb_k, cache_sb_v, state_ffn_conv, p_prompt, p_sample, norm_pre_mix, w_in, ssd_conv_w, ssd_conv_b, ssd_dt_bias, ssd_a_log, ssd_d, ssd_norm, w_br_ssd, w_br_sb, w_out, norm_post_mix, norm_pre_ffn, w_up, ffn_conv_w, ffn_conv_b, w_down, norm_post_ffn, w_ple, w_ple_gate, norm_ple):
    weights = dict(norm_pre_mix=norm_pre_mix, w_in=w_in, ssd_conv_w=ssd_conv_w, ssd_conv_b=ssd_conv_b,
                   ssd_dt_bias=ssd_dt_bias, ssd_a_log=ssd_a_log, ssd_d=ssd_d, ssd_norm=ssd_norm,
                   w_br_ssd=w_br_ssd, w_br_sb=w_br_sb, w_out=w_out, norm_post_mix=norm_post_mix,
                   norm_pre_ffn=norm_pre_ffn, w_up=w_up, ffn_conv_w=ffn_conv_w, ffn_conv_b=ffn_conv_b,
                   w_down=w_down, norm_post_ffn=norm_post_ffn, w_ple=w_ple, w_ple_gate=w_ple_gate,
                   norm_ple=norm_ple)
    bp = x_prompt.shape[0]
    zero_conv = jnp.zeros((DEPTH, bp, SSD_CONV - 1, SSD_CONV_DIM), F32)
    zero_ssm = jnp.zeros((DEPTH, bp, SSD_HEADS, SSD_HEAD_DIM, SSD_STATE), F32)
    zero_ffn = jnp.zeros((DEPTH, bp, FFN_CONV - 1, D_FF), F32)
    y_prompt, ps = _run(x_prompt, p_prompt, zero_conv, zero_ssm, None, None, zero_ffn, weights)
    y_sample, ss = _run(x_sample, p_sample, state_ssd_conv, state_ssd, cache_sb_k, cache_sb_v,
                        state_ffn_conv, weights)
    return (y_prompt, y_sample, ps[0], ps[1], ps[2], ps[3], ps[4], ss[0], ss[1], ss[2], ss[3], ss[4])
```

```python
import functools

import jax
import jax.numpy as jnp
from jax import lax
from jax.experimental import pallas as pl
from jax.experimental.pallas import tpu as pltpu

F32 = jnp.float32
BF16 = jnp.bfloat16

D_MODEL = 1024
DEPTH = 2
CHUNK = 64
D_SSM = 2048
SSD_HEAD_DIM = 64
SSD_HEADS = 32
SSD_GROUPS = 4
HEADS_PER_GROUP = 8
SSD_STATE = 128
SSD_CONV = 4
SSD_CONV_DIM = D_SSM + 2 * SSD_GROUPS * SSD_STATE
GROUP_WIDTH = D_SSM // SSD_GROUPS
SB_HEAD_DIM = 128
SB_HEADS = 8
SB_WIDTH = 1024
SB_BLOCK = 128
D_FF = 4096
FFN_CONV = 3
PLE_DIM = 256
EPS = 1e-6
GELU_C = 0.7978845608028654
GELU_A = 0.044715
SB_SCALE = SB_HEAD_DIM ** -0.5
SB_HEAD_GROUP = 8
SB_DEAD_LOG = 120.0

LANES = 128
SUBLANES = 8
VMEM_LIMIT = 56 * 1024 * 1024
FFN_ROWS = 1024


def _cparams(sem):
    return pltpu.CompilerParams(dimension_semantics=sem, vmem_limit_bytes=VMEM_LIMIT)


def _rms(x, gain):
    return x * lax.rsqrt(jnp.mean(x * x, axis=-1, keepdims=True) + EPS) * gain


def _sigmoid(x):
    return 0.5 + 0.5 * jnp.tanh(0.5 * x)


def _silu(x):
    h = 0.5 * x
    return h + h * jnp.tanh(h)


def _dot(a, b):
    return jnp.dot(a, b, preferred_element_type=F32)


def _dot_nt(a, b):
    return lax.dot_general(a, b, (((1,), (1,)), ((), ())), preferred_element_type=F32)


def _dot_tn(a, b):
    return lax.dot_general(a, b, (((0,), (0,)), ((), ())), preferred_element_type=F32)


def _split_bf16(v, n):
    parts = []
    r = v
    for _ in range(n):
        p = r.astype(BF16)
        parts.append(p)
        r = r - p.astype(F32)
    return parts


def _shifted_rows(x, hist, k):
    nseq = hist.shape[0]
    tl = x.shape[0] // nseq
    rolled = pltpu.roll(x, k, 0)
    row = lax.broadcasted_iota(jnp.int32, (SUBLANES, x.shape[1]), 0)
    pieces = []
    for b in range(nseq):
        start = b * tl
        pieces.append(jnp.where(row < k, pltpu.roll(hist[b], k, 0), rolled[start:start + SUBLANES]))
        if tl > SUBLANES:
            pieces.append(rolled[start + SUBLANES:start + tl])
    return pieces[0] if len(pieces) == 1 else jnp.concatenate(pieces, axis=0)


def _causal_conv(x, hist, w, b):
    width = w.shape[0]
    acc = b
    for tap in range(width):
        k = width - 1 - tap
        sh = x if k == 0 else _shifted_rows(x, hist, k)
        acc = acc + sh * w[tap:tap + 1, :]
    return acc


def _norm_matmul_kernel(x_ref, g_ref, w_ref, o_ref, h_ref, *, normalize):
    @pl.when(pl.program_id(1) == 0)
    def _():
        x = x_ref[...]
        if normalize:
            x = _rms(x, g_ref[...])
        h_ref[...] = x.astype(BF16)

    o_ref[...] = _dot(h_ref[...], w_ref[...])


def _norm_matmul(x, gain, w, *, normalize=True):
    m, k = x.shape
    n = w.shape[1]
    tm = min(1024, m)
    tn = min(1024, n)
    return pl.pallas_call(
        functools.partial(_norm_matmul_kernel, normalize=normalize),
        out_shape=jax.ShapeDtypeStruct((m, n), F32),
        grid=(m // tm, n // tn),
        in_specs=[pl.BlockSpec((tm, k), lambda i, j: (i, 0)),
                  pl.BlockSpec((1, k), lambda i, j: (0, 0)),
                  pl.BlockSpec((k, tn), lambda i, j: (0, j))],
        out_specs=pl.BlockSpec((tm, tn), lambda i, j: (i, j)),
        scratch_shapes=[pltpu.VMEM((tm, k), BF16)],
        compiler_params=_cparams(("parallel", "arbitrary")),
        name="norm_matmul",
    )(x, gain, w)


PROJ_TN = 1024
_PROJ_WIDTHS = (2 * D_SSM, SSD_CONV_DIM, SB_WIDTH, SB_WIDTH, SB_WIDTH)
_PROJ_FIRST = tuple(sum(_PROJ_WIDTHS[:n]) // PROJ_TN for n in range(len(_PROJ_WIDTHS) + 1))


def _in_proj_kernel(x_ref, g_ref, w_ref, zg_ref, xbc_ref, q_ref, k_ref, v_ref, kvb_ref, h_ref):
    j = pl.program_id(1)

    @pl.when(j == 0)
    def _():
        h_ref[...] = _rms(x_ref[...], g_ref[...]).astype(BF16)

    def owns(n):
        return jnp.logical_and(j >= _PROJ_FIRST[n], j < _PROJ_FIRST[n + 1])

    def proj():
        return _dot(h_ref[...], w_ref[...])

    @pl.when(owns(0))
    def _():
        zg_ref[...] = proj().astype(BF16)

    @pl.when(owns(1))
    def _():
        xbc_ref[...] = proj().astype(BF16)

    @pl.when(owns(2))
    def _():
        q_ref[...] = proj().astype(BF16)

    heads_per_tile = PROJ_TN // SB_HEAD_DIM
    tm = x_ref.shape[0]
    for n, rows_ref in ((3, k_ref), (4, v_ref)):
        for t in range(_PROJ_FIRST[n + 1] - _PROJ_FIRST[n]):
            @pl.when(j == _PROJ_FIRST[n] + t)
            def _(rows_ref=rows_ref, t=t):
                r = proj()
                kvb_ref[...] = r.astype(BF16)
                for hh in range(heads_per_tile):
                    head = t * heads_per_tile + hh
                    rows_ref[pl.ds(head, tm, stride=SB_HEADS), :] = r[:, hh * SB_HEAD_DIM:(hh + 1) * SB_HEAD_DIM]


def _in_proj(x, gain, w):
    m, k = x.shape
    tm = min(1024, m)
    tn = PROJ_TN
    bf16 = lambda n: jax.ShapeDtypeStruct((m, n), BF16)
    head_rows = jax.ShapeDtypeStruct((m * SB_HEADS, SB_HEAD_DIM), F32)

    def tile_of(n_first, n_last):
        first, count = _PROJ_FIRST[n_first], _PROJ_FIRST[n_last + 1] - _PROJ_FIRST[n_first]
        return pl.BlockSpec((tm, tn), lambda i, j: (i, jnp.clip(j - first, 0, count - 1)))

    rows_spec = pl.BlockSpec((tm * SB_HEADS, SB_HEAD_DIM), lambda i, j: (i, 0))
    return pl.pallas_call(
        _in_proj_kernel,
        out_shape=(bf16(2 * D_SSM), bf16(SSD_CONV_DIM), bf16(SB_WIDTH), head_rows, head_rows, bf16(2 * SB_WIDTH)),
        grid=(m // tm, _PROJ_FIRST[-1]),
        in_specs=[pl.BlockSpec((tm, k), lambda i, j: (i, 0)),
                  pl.BlockSpec((1, k), lambda i, j: (0, 0)),
                  pl.BlockSpec((k, tn), lambda i, j: (0, j))],
        out_specs=(tile_of(0, 0), tile_of(1, 1), tile_of(2, 2), rows_spec, rows_spec, tile_of(3, 4)),
        scratch_shapes=[pltpu.VMEM((tm, k), BF16)],
        compiler_params=_cparams(("parallel", "arbitrary")),
        name="in_proj",
    )(x, gain, w)


def _ssd_kernel(xbc_ref, dt_ref, cst_ref, cw_ref, cb_ref, dtb_ref, alog_ref, dexp_ref, h0_ref, e_ref, tri_ref,
                shift_ref, y_ref, hout_ref, h_scr, carry_scr, xc_scr, dt_scr, *, cl, nchunk):
    l = pl.program_id(1)

    @pl.when(l == 0)
    def _():
        h_scr[...] = h0_ref[0]
        carry_scr[...] = cst_ref[0]

    xb = xbc_ref[0]
    xt = xb.astype(F32)
    tl = xt.shape[0]
    moved = _dot(shift_ref[...], xb)
    hist = carry_scr[...]
    row = lax.broadcasted_iota(jnp.int32, (SUBLANES, SSD_CONV_DIM), 0)
    conv = cb_ref[...] + xt * cw_ref[SSD_CONV - 1:SSD_CONV, :]
    head = jnp.zeros((SUBLANES, SSD_CONV_DIM), F32)
    for back in range(1, SSD_CONV):
        w_tap = cw_ref[SSD_CONV - 1 - back:SSD_CONV - back, :]
        conv = conv + moved[(back - 1) * tl:back * tl] * w_tap
        head = head + jnp.where(row < back, pltpu.roll(hist, back, 0), 0.0) * w_tap
    carry_scr[...] = xt[tl - SUBLANES:tl]
    xc_scr[...] = _silu(conv)
    xc_scr[0:SUBLANES, :] = _silu(conv[0:SUBLANES] + head)
    dtr = dt_ref[0] + dtb_ref[...]
    dt_scr[...] = jnp.maximum(dtr, 0.0) + jnp.log1p(jnp.exp(-jnp.abs(dtr)))

    a_head = -jnp.exp(alog_ref[...])
    tri = tri_ref[...]
    expand = e_ref[...]
    rowi = lax.broadcasted_iota(jnp.int32, (cl, cl), 0)
    coli = lax.broadcasted_iota(jnp.int32, (cl, cl), 1)
    causal = coli <= rowi
    lane = lax.broadcasted_iota(jnp.int32, (1, LANES), 1)
    half_masks = ((lane < SSD_HEAD_DIM).astype(F32), (lane >= SSD_HEAD_DIM).astype(F32))

    def expand_heads(v):
        return _dot(v.astype(BF16), expand)

    def chunk(r0):
        rows = pl.ds(r0, cl)
        xs = xc_scr[rows, 0:D_SSM]
        bm = xc_scr[rows, D_SSM:D_SSM + SSD_GROUPS * SSD_STATE]
        cm = xc_scr[rows, D_SSM + SSD_GROUPS * SSD_STATE:SSD_CONV_DIM]
        dtc = dt_scr[rows, :]
        a = dtc * a_head
        acum = sum(_dot(tri, p) for p in _split_bf16(a, 3))
        if cl < LANES:
            acum_sq = jnp.concatenate([acum, jnp.zeros((LANES - cl, LANES), F32)], axis=0)
        else:
            acum_sq = acum
        acum_t = acum_sq.T
        a_end = acum[cl - 1:cl, :]
        dt_e = expand_heads(dtc)
        dec_end_e = expand_heads(jnp.exp(a_end - acum))
        dec_in_e = expand_heads(jnp.exp(acum))
        chunk_decay = jnp.broadcast_to(jnp.exp(acum_t[:, cl - 1:cl]), (LANES, LANES))
        x_dt = xs * dt_e
        x_end = x_dt * dec_end_e
        groups = range(SSD_GROUPS)
        heads = range(SSD_HEADS)
        gss = [slice(g * SSD_STATE, (g + 1) * SSD_STATE) for g in groups]
        gws = [slice(g * GROUP_WIDTH, (g + 1) * GROUP_WIDTH) for g in groups]
        bgs = [bm[:, gs].astype(BF16) for gs in gss]
        cgs = [cm[:, gs].astype(BF16) for gs in gss]
        cbs = [_dot_nt(cg, bg) for cg, bg in zip(cgs, bgs)]
        y_offs = [_dot_nt(cg, h_scr[gw, :].astype(BF16)) for cg, gw in zip(cgs, gws)]
        sts = [_dot_tn(x_end[:, gw].astype(BF16), bg) for gw, bg in zip(gws, bgs)]
        m_es = []
        for e in heads:
            seg = acum[:, e:e + 1] - acum_t[e:e + 1, 0:cl]
            dec = jnp.exp(jnp.where(causal, seg, -jnp.inf))
            m_es.append((cbs[e // HEADS_PER_GROUP] * dec).astype(BF16))
        for pr in range(SSD_HEADS // 2):
            g = pr // (HEADS_PER_GROUP // 2)
            ps = slice(pr * LANES, (pr + 1) * LANES)
            pg = slice(pr * LANES - g * GROUP_WIDTH, (pr + 1) * LANES - g * GROUP_WIDTH)
            xp = x_dt[:, ps]
            yp = y_offs[g][:, pg] * dec_in_e[:, ps]
            for sub in range(2):
                yp = yp + _dot(m_es[2 * pr + sub], (xp * half_masks[sub]).astype(BF16))
            y_ref[0, rows, ps] = yp + dexp_ref[:, ps] * xs[:, ps]
        for e in heads:
            g = e // HEADS_PER_GROUP
            hs = slice(e * SSD_HEAD_DIM, (e + 1) * SSD_HEAD_DIM)
            ls = slice(e * SSD_HEAD_DIM - g * GROUP_WIDTH, (e + 1) * SSD_HEAD_DIM - g * GROUP_WIDTH)
            h_scr[hs, :] = h_scr[hs, :] * chunk_decay[e:e + 1, :] + sts[g][ls, :]

    if nchunk == 1:
        chunk(0)
    else:
        def body(c, carry):
            chunk(pl.multiple_of(c * cl, cl))
            return carry
        lax.fori_loop(0, nchunk, body, 0)

    @pl.when(l == pl.num_programs(1) - 1)
    def _():
        hout_ref[0] = h_scr[...]


def _ssd(xbc, dt_raw, conv_state8, conv_w, conv_b, dt_bias, a_log, d_exp, h0, expand, tri, *, tl, cl):
    bsz, length, _ = xbc.shape
    kernel = functools.partial(_ssd_kernel, cl=cl, nchunk=tl // cl)
    full2 = lambda b, l: (0, 0)
    src = jnp.arange(tl)[None, None, :]
    dst = jnp.arange(tl)[None, :, None]
    back = jnp.arange(1, SSD_CONV)[:, None, None]
    shift = (src == dst - back).astype(BF16).reshape((SSD_CONV - 1) * tl, tl)
    return pl.pallas_call(
        kernel,
        out_shape=(jax.ShapeDtypeStruct((bsz, length, D_SSM), F32),
                   jax.ShapeDtypeStruct((bsz, D_SSM, SSD_STATE), F32)),
        grid=(bsz, length // tl),
        in_specs=[pl.BlockSpec((1, tl, SSD_CONV_DIM), lambda b, l: (b, l, 0)),
                  pl.BlockSpec((1, tl, LANES), lambda b, l: (b, l, 0)),
                  pl.BlockSpec((1, SUBLANES, SSD_CONV_DIM), lambda b, l: (b, 0, 0)),
                  pl.BlockSpec((SSD_CONV, SSD_CONV_DIM), full2),
                  pl.BlockSpec((1, SSD_CONV_DIM), full2),
                  pl.BlockSpec((1, LANES), full2),
                  pl.BlockSpec((1, LANES), full2),
                  pl.BlockSpec((1, D_SSM), full2),
                  pl.BlockSpec((1, D_SSM, SSD_STATE), lambda b, l: (b, 0, 0)),
                  pl.BlockSpec((LANES, D_SSM), full2),
                  pl.BlockSpec((cl, cl), full2),
                  pl.BlockSpec(((SSD_CONV - 1) * tl, tl), full2)],
        out_specs=(pl.BlockSpec((1, tl, D_SSM), lambda b, l: (b, l, 0)),
                   pl.BlockSpec((1, D_SSM, SSD_STATE), lambda b, l: (b, 0, 0))),
        scratch_shapes=[pltpu.VMEM((D_SSM, SSD_STATE), F32),
                        pltpu.VMEM((SUBLANES, SSD_CONV_DIM), F32),
                        pltpu.VMEM((tl, SSD_CONV_DIM), F32),
                        pltpu.VMEM((tl, LANES), F32)],
        compiler_params=_cparams(("parallel", "arbitrary")),
        name="ssd",
    )(xbc, dt_raw, conv_state8, conv_w, conv_b, dt_bias, a_log, d_exp, h0, expand, tri, shift)


def _sb_blocks(qs, blocks, laters, accs, tri_ext):
    zs = [[_dot_nt(q, kb) * SB_SCALE for q, kb in zip(qs, kbs)] for kbs, _, _ in blocks]
    log_keeps, log_betas = [], []
    for z_row, (_, _, earlier) in zip(zs, blocks):
        keep_row, beta_row = [], []
        for z in z_row:
            l1p = jnp.log(1.0 + jnp.exp(-jnp.abs(z)))
            log_keep = -(jnp.maximum(z, 0.0) + l1p)
            if earlier is not None:
                log_keep = jnp.where(earlier, log_keep, 0.0)
            keep_row.append(log_keep)
            beta_row.append(jnp.minimum(z, 0.0) - l1p)
        log_keeps.append(keep_row)
        log_betas.append(beta_row)
    exts = [[_dot(lk.astype(BF16), tri_ext) for lk in keep_row] for keep_row in log_keeps]
    ws = []
    for beta_row, ext_row, (_, _, earlier) in zip(log_betas, exts, blocks):
        w_row = []
        for log_beta, ext, later in zip(beta_row, ext_row, laters):
            w = jnp.exp(log_beta + ext[:, :SB_BLOCK] + later)
            if earlier is not None:
                w = jnp.where(earlier, w, 0.0)
            w_row.append(w.astype(BF16))
        ws.append(w_row)
        laters = [later + ext[:, SB_BLOCK:] for later, ext in zip(laters, ext_row)]
    for w_row, (_, vbs, _) in zip(ws, blocks):
        accs = [acc + _dot(w, vb) for acc, w, vb in zip(accs, w_row, vbs)]
    return laters, accs


def _sb_cols(h):
    return slice(h * SB_HEAD_DIM, (h + 1) * SB_HEAD_DIM)


def _sb_alive(later_scr):
    return jnp.max(later_scr[...]) > -SB_DEAD_LOG


def _sb_heads(q_ref, blocks, later_scr, acc_scr, tri_ext, init):
    tq = q_ref.shape[1]
    for h0 in range(0, SB_HEADS, SB_HEAD_GROUP):
        heads = range(h0, h0 + SB_HEAD_GROUP)
        qs = [q_ref[0, :, _sb_cols(h)] for h in heads]
        if init:
            laters = [jnp.zeros((tq, SB_BLOCK), F32) for _ in heads]
            accs = [jnp.zeros((tq, SB_HEAD_DIM), F32) for _ in heads]
        else:
            laters = [later_scr[h] for h in heads]
            accs = [acc_scr[h] for h in heads]
        per_head = [([k_of(h) for h in heads], [v_of(h) for h in heads], earlier)
                    for k_of, v_of, earlier in blocks]
        laters, accs = _sb_blocks(qs, per_head, laters, accs, tri_ext)
        for h, later, acc in zip(heads, laters, accs):
            later_scr[h] = later
            acc_scr[h] = acc


def _sb_kv_of(read):
    return (lambda h: read(_sb_cols(h))), (lambda h: read(_sb_cols(SB_HEADS + h)))


def _sb_diag_mask(tq):
    key = lax.broadcasted_iota(jnp.int32, (tq, SB_BLOCK), 1)
    query = lax.broadcasted_iota(jnp.int32, (tq, SB_BLOCK), 0)
    return key < query


def _sb_prompt_kernel(q_ref, kv0_ref, kv1_ref, kv2_ref, kv_hbm, tri_ref, o_ref,
                      later_scr, acc_scr, kv_buf, sem):
    b = pl.program_id(0)
    qi = pl.program_id(1)
    tri_ext = tri_ref[...]
    state = (later_scr, acc_scr, tri_ext)

    k0, v0 = _sb_kv_of(lambda c: kv0_ref[0, :, c])
    diagonal = (k0, v0, _sb_diag_mask(q_ref.shape[1]))

    @pl.when(qi >= 2)
    def _():
        k1, v1 = _sb_kv_of(lambda c: kv1_ref[0, :, c])
        k2, v2 = _sb_kv_of(lambda c: kv2_ref[0, :, c])
        _sb_heads(q_ref, [diagonal, (k1, v1, None), (k2, v2, None)], *state, init=True)

    @pl.when(qi < 2)
    def _():
        _sb_heads(q_ref, [diagonal], *state, init=True)

    def more(carry):
        j, alive = carry
        return jnp.logical_and(j >= 0, alive)

    def body(carry):
        j, _ = carry
        rows = pl.ds(pl.multiple_of(j * SB_BLOCK, SB_BLOCK), SB_BLOCK)
        copy = pltpu.make_async_copy(kv_hbm.at[b, rows, :], kv_buf, sem)
        copy.start()
        copy.wait()
        kb, vb = _sb_kv_of(lambda c: kv_buf[:, c])
        _sb_heads(q_ref, [(kb, vb, None)], *state, init=False)
        return j - 1, _sb_alive(later_scr)

    lax.while_loop(more, body, (jnp.where(qi >= 2, qi - 3, qi - 1), _sb_alive(later_scr)))
    for h in range(SB_HEADS):
        o_ref[0, :, _sb_cols(h)] = acc_scr[h].astype(o_ref.dtype)


def _sb_prompt(q, kvb, tri_ext):
    bsz, length, _ = q.shape
    tq = SB_BLOCK
    blk = lambda back: pl.BlockSpec((1, SB_BLOCK, 2 * SB_WIDTH), lambda b, i: (b, jnp.maximum(i - back, 0), 0))
    return pl.pallas_call(
        _sb_prompt_kernel,
        out_shape=jax.ShapeDtypeStruct((bsz, length, SB_WIDTH), BF16),
        grid=(bsz, length // tq),
        in_specs=[pl.BlockSpec((1, tq, SB_WIDTH), lambda b, i: (b, i, 0)),
                  blk(0), blk(1), blk(2),
                  pl.BlockSpec(memory_space=pl.ANY),
                  pl.BlockSpec((SB_BLOCK, 2 * SB_BLOCK), lambda b, i: (0, 0))],
        out_specs=pl.BlockSpec((1, tq, SB_WIDTH), lambda b, i: (b, i, 0)),
        scratch_shapes=[pltpu.VMEM((SB_HEADS, tq, SB_BLOCK), F32),
                        pltpu.VMEM((SB_HEADS, tq, SB_HEAD_DIM), F32),
                        pltpu.VMEM((SB_BLOCK, 2 * SB_WIDTH), BF16),
                        pltpu.SemaphoreType.DMA(())],
        compiler_params=_cparams(("parallel", "arbitrary")),
        name="sb_prompt",
    )(q, kvb, kvb, kvb, kvb, tri_ext)


SB_BLOCK_ROWS = SB_BLOCK * SB_HEADS
SB_RECENT_BLOCKS = 2


def _sb_cache_heads(read):
    return lambda first_row: (lambda h: read(pl.ds(first_row + h, SB_BLOCK, stride=SB_HEADS)).astype(BF16))


def _sb_sample_kernel(q_ref, kvn_ref, kr_ref, vr_ref, k_hbm, v_hbm, tri_ref, o_ref,
                      later_scr, acc_scr, k_buf, v_buf, sem, *, npast, first):
    b = pl.program_id(0)
    tri_ext = tri_ref[...]
    state = (later_scr, acc_scr, tri_ext)
    kn, vn = _sb_kv_of(lambda c: kvn_ref[0, :, c])
    k_recent = _sb_cache_heads(lambda rows: kr_ref[0, rows, :])
    v_recent = _sb_cache_heads(lambda rows: vr_ref[0, rows, :])
    recent = [(k_recent(t * SB_BLOCK_ROWS), v_recent(t * SB_BLOCK_ROWS), None)
              for t in reversed(range(SB_RECENT_BLOCKS))]
    _sb_heads(q_ref, [(kn, vn, _sb_diag_mask(q_ref.shape[1]))] + recent, *state, init=True)

    def more(carry):
        j, alive = carry
        return jnp.logical_and(j >= 0, alive)

    def body(carry):
        j, _ = carry
        rows = pl.ds(pl.multiple_of(j * SB_BLOCK_ROWS, SB_BLOCK_ROWS), SB_BLOCK_ROWS)
        copies = [pltpu.make_async_copy(hbm.at[first + b, rows, :], buf, sem.at[n])
                  for n, (hbm, buf) in enumerate(((k_hbm, k_buf), (v_hbm, v_buf)))]
        for copy in copies:
            copy.start()
        for copy in copies:
            copy.wait()
        kb = _sb_cache_heads(lambda r: k_buf[r, :])(0)
        vb = _sb_cache_heads(lambda r: v_buf[r, :])(0)
        _sb_heads(q_ref, [(kb, vb, None)], *state, init=False)
        return j - 1, _sb_alive(later_scr)

    lax.while_loop(more, body, (npast - SB_RECENT_BLOCKS - 1, _sb_alive(later_scr)))
    for h in range(SB_HEADS):
        o_ref[0, :, _sb_cols(h)] = acc_scr[h].astype(o_ref.dtype)


def _sb_sample(q, kvb_new_pad, k_past, v_past, tri_ext, layer):
    bsz, lq, _ = q.shape
    rows = k_past.shape[1]
    npast = rows // SB_BLOCK_ROWS
    assert rows % SB_BLOCK_ROWS == 0 and npast >= SB_RECENT_BLOCKS and lq <= SB_BLOCK
    first = layer * bsz
    recent_rows = SB_RECENT_BLOCKS * SB_BLOCK_ROWS
    assert rows % recent_rows == 0
    recent = pl.BlockSpec((1, recent_rows, SB_HEAD_DIM), lambda b: (first + b, rows // recent_rows - 1, 0))
    return pl.pallas_call(
        functools.partial(_sb_sample_kernel, npast=npast, first=first),
        out_shape=jax.ShapeDtypeStruct((bsz, lq, SB_WIDTH), BF16),
        grid=(bsz,),
        in_specs=[pl.BlockSpec((1, lq, SB_WIDTH), lambda b: (b, 0, 0)),
                  pl.BlockSpec((1, SB_BLOCK, 2 * SB_WIDTH), lambda b: (b, 0, 0)),
                  recent, recent,
                  pl.BlockSpec(memory_space=pl.ANY),
                  pl.BlockSpec(memory_space=pl.ANY),
                  pl.BlockSpec((SB_BLOCK, 2 * SB_BLOCK), lambda b: (0, 0))],
        out_specs=pl.BlockSpec((1, lq, SB_WIDTH), lambda b: (b, 0, 0)),
        scratch_shapes=[pltpu.VMEM((SB_HEADS, lq, SB_BLOCK), F32),
                        pltpu.VMEM((SB_HEADS, lq, SB_HEAD_DIM), F32),
                        pltpu.VMEM((SB_BLOCK_ROWS, SB_HEAD_DIM), F32),
                        pltpu.VMEM((SB_BLOCK_ROWS, SB_HEAD_DIM), F32),
                        pltpu.SemaphoreType.DMA((2,))],
        compiler_params=_cparams(("parallel",)),
        name="sb_sample",
    )(q, kvb_new_pad, k_past, v_past, k_past, v_past, tri_ext)


def _merge_kernel(y_ref, z_ref, gate_ref, attn_ref, x_ref, nw_ref, w1_ref, w2_ref, w3_ref, gpost_ref, o_ref):
    z = z_ref[...].astype(F32)
    g = y_ref[...] * _silu(z)
    b1 = None
    for k in range(SSD_GROUPS):
        gw = slice(k * GROUP_WIDTH, (k + 1) * GROUP_WIDTH)
        gk = _rms(g[:, gw], nw_ref[:, gw])
        part = _dot(gk.astype(BF16), w1_ref[gw, :])
        b1 = part if b1 is None else b1 + part
    b2 = _dot(attn_ref[...], w2_ref[...])
    gl = gate_ref[...].astype(F32)
    m = _sigmoid(gl[:, :D_MODEL]) * b1 + _sigmoid(gl[:, D_MODEL:]) * b2
    mixed = _dot(m.astype(BF16), w3_ref[...])
    o_ref[...] = x_ref[...] + _rms(mixed, gpost_ref[...])


def _merge(y, zg, attn, x, ssd_norm, w_br_ssd, w_br_sb, w_out, norm_post):
    m = x.shape[0]
    tm = min(256, m)
    row = lambda i: (i, 0)
    full = lambda i: (0, 0)
    return pl.pallas_call(
        _merge_kernel,
        out_shape=jax.ShapeDtypeStruct((m, D_MODEL), F32),
        grid=(m // tm,),
        in_specs=[pl.BlockSpec((tm, D_SSM), row),
                  pl.BlockSpec((tm, D_SSM), row),
                  pl.BlockSpec((tm, 2 * D_MODEL), lambda i: (i, 1)),
                  pl.BlockSpec((tm, SB_WIDTH), row),
                  pl.BlockSpec((tm, D_MODEL), row),
                  pl.BlockSpec((1, D_SSM), full),
                  pl.BlockSpec((D_SSM, D_MODEL), full),
                  pl.BlockSpec((SB_WIDTH, D_MODEL), full),
                  pl.BlockSpec((D_MODEL, D_MODEL), full),
                  pl.BlockSpec((1, D_MODEL), full)],
        out_specs=pl.BlockSpec((tm, D_MODEL), row),
        compiler_params=_cparams(("parallel",)),
        name="merge",
    )(y, zg, zg, attn, x, ssd_norm, w_br_ssd, w_br_sb, w_out, norm_post)


def _ffn_kernel(x_ref, st_ref, gpre_ref, wg_ref, wu_ref, cw_ref, cb_ref, wd_ref, gpost_ref,
                o_ref, nst_ref, h_scr, acc_scr, carry_scr):
    l = pl.program_id(1)
    j = pl.program_id(2)
    nb, tl, _ = x_ref.shape

    @pl.when(j == 0)
    def _():
        h_scr[...] = _rms(x_ref[...].reshape(nb * tl, D_MODEL), gpre_ref[...]).astype(BF16)
        acc_scr[...] = jnp.zeros_like(acc_scr)

    @pl.when(l == 0)
    def _():
        carry_scr[j] = st_ref[...]

    h = h_scr[...]
    gate = _dot(h, wg_ref[...])
    up = _dot(h, wu_ref[...])
    conv = _causal_conv(gate, carry_scr[j], cw_ref[...], cb_ref[...])
    tail = gate.reshape(nb, tl, gate.shape[1])[:, tl - SUBLANES:, :]
    carry_scr[j] = tail

    @pl.when(l == pl.num_programs(1) - 1)
    def _():
        nst_ref[...] = tail

    half = 0.5 * conv
    gelu = half + half * jnp.tanh(conv * (GELU_C + (GELU_C * GELU_A) * (conv * conv)))
    acc_scr[...] += _dot((gelu * up).astype(BF16), wd_ref[...])

    @pl.when(j == pl.num_programs(2) - 1)
    def _():
        out = x_ref[...].reshape(nb * tl, D_MODEL) + _rms(acc_scr[...], gpost_ref[...])
        o_ref[...] = out.reshape(nb, tl, D_MODEL)


def _ffn(x, state8, norm_pre, w_up, conv_w, conv_b, w_down, norm_post, *, nb, tl, fc):
    bsz, length, _ = x.shape
    nf = D_FF // fc
    nl = length // tl
    full = lambda b, l, j: (0, 0)
    nst_map = lambda b, l, j: (b, 0, jnp.where(l == nl - 1, j, 0))
    return pl.pallas_call(
        _ffn_kernel,
        out_shape=(jax.ShapeDtypeStruct((bsz, length, D_MODEL), F32),
                   jax.ShapeDtypeStruct((bsz, SUBLANES, D_FF), F32)),
        grid=(bsz // nb, nl, nf),
        in_specs=[pl.BlockSpec((nb, tl, D_MODEL), lambda b, l, j: (b, l, 0)),
                  pl.BlockSpec((nb, SUBLANES, fc), lambda b, l, j: (b, 0, j)),
                  pl.BlockSpec((1, D_MODEL), full),
                  pl.BlockSpec((D_MODEL, fc), lambda b, l, j: (0, j)),
                  pl.BlockSpec((D_MODEL, fc), lambda b, l, j: (0, nf + j)),
                  pl.BlockSpec((FFN_CONV, fc), lambda b, l, j: (0, j)),
                  pl.BlockSpec((1, fc), lambda b, l, j: (0, j)),
                  pl.BlockSpec((fc, D_MODEL), lambda b, l, j: (j, 0)),
                  pl.BlockSpec((1, D_MODEL), full)],
        out_specs=(pl.BlockSpec((nb, tl, D_MODEL), lambda b, l, j: (b, l, 0)),
                   pl.BlockSpec((nb, SUBLANES, fc), nst_map)),
        scratch_shapes=[pltpu.VMEM((nb * tl, D_MODEL), BF16),
                        pltpu.VMEM((nb * tl, D_MODEL), F32),
                        pltpu.VMEM((nf, nb, SUBLANES, fc), F32)],
        compiler_params=_cparams(("parallel", "arbitrary", "arbitrary")),
        name="ffn",
    )(x, state8, norm_pre, w_up, w_up, conv_w, conv_b, w_down, norm_post)


def _ple_kernel(x_ref, p_ref, wp_ref, wg_ref, g_ref, o_ref):
    x = x_ref[...]
    ple = _dot(p_ref[...].astype(BF16), wp_ref[...]) * _sigmoid(_dot(x.astype(BF16), wg_ref[...]))
    o_ref[...] = x + _rms(ple, g_ref[...])


def _ple(x, p, w_ple, w_gate, gain):
    m = x.shape[0]
    tm = min(512, m)
    row = lambda i: (i, 0)
    full = lambda i: (0, 0)
    return pl.pallas_call(
        _ple_kernel,
        out_shape=jax.ShapeDtypeStruct((m, D_MODEL), F32),
        grid=(m // tm,),
        in_specs=[pl.BlockSpec((tm, D_MODEL), row),
                  pl.BlockSpec((tm, PLE_DIM), row),
                  pl.BlockSpec((PLE_DIM, D_MODEL), full),
                  pl.BlockSpec((D_MODEL, D_MODEL), full),
                  pl.BlockSpec((1, D_MODEL), full)],
        out_specs=pl.BlockSpec((tm, D_MODEL), row),
        compiler_params=_cparams(("parallel",)),
        name="ple",
    )(x, p, w_ple, w_gate, gain)


def _pad_rows_to8(state):
    return jnp.pad(state, ((0, 0), (SUBLANES - state.shape[1], 0), (0, 0)))


def _pad_lanes(v):
    return jnp.pad(v, (0, LANES - v.shape[0]))[None, :]


def _constants(cl):
    head = jnp.arange(LANES)[:, None]
    chan = jnp.arange(D_SSM)[None, :] // SSD_HEAD_DIM
    expand = (head == chan).astype(BF16)
    idx = jnp.arange(cl)
    tri_cum = (idx[None, :] <= idx[:, None]).astype(BF16)
    kidx = jnp.arange(SB_BLOCK)
    tri_in = (kidx[:, None] > kidx[None, :]).astype(BF16)
    tri_ext = jnp.concatenate([tri_in, jnp.ones((SB_BLOCK, SB_BLOCK), BF16)], axis=1)
    return expand, tri_cum, tri_ext


def _layer(x, p, conv_st, ssm_st, k_cache, v_cache, ffn_st, lw, layer):
    bsz, length, _ = x.shape
    m = bsz * length
    x2 = x.reshape(m, D_MODEL)
    cl = CHUNK if length >= CHUNK else length
    expand, tri_cum, tri_ext = _constants(cl)

    w_in = lw['w_in']
    o_z, o_xbc, o_dt, o_q, o_g = 0, D_SSM, D_SSM + SSD_CONV_DIM, D_SSM + SSD_CONV_DIM + SSD_HEADS, \
        D_SSM + SSD_CONV_DIM + SSD_HEADS + 3 * SB_WIDTH
    w_main = jnp.concatenate([w_in[:, o_z:o_xbc], w_in[:, o_g:], w_in[:, o_xbc:o_dt], w_in[:, o_q:o_g]],
                             axis=1).astype(BF16)
    w_dt = jnp.pad(w_in[:, o_dt:o_q], ((0, 0), (0, LANES - SSD_HEADS))).astype(BF16)
    g_pre = lw['norm_pre_mix'][None, :]
    zg, xbc, q, k_rows, v_rows, kvb = _in_proj(x2, g_pre, w_main)
    dt_raw = _norm_matmul(x2, g_pre, w_dt)

    xbc3 = xbc.reshape(bsz, length, SSD_CONV_DIM)
    tl = min(256, length)
    y, h_new = _ssd(xbc3, dt_raw.reshape(bsz, length, LANES), _pad_rows_to8(conv_st),
                    lw['ssd_conv_w'], lw['ssd_conv_b'][None, :], _pad_lanes(lw['ssd_dt_bias']),
                    _pad_lanes(lw['ssd_a_log']), jnp.repeat(lw['ssd_d'], SSD_HEAD_DIM)[None, :],
                    ssm_st.reshape(bsz, D_SSM, SSD_STATE), expand, tri_cum, tl=tl, cl=cl)
    tail = 2 * SUBLANES
    x_tail = x[:, length - tail:, :].reshape(bsz * tail, D_MODEL)
    xbc_tail = _norm_matmul(x_tail, g_pre, w_in[:, o_xbc:o_dt].astype(BF16))
    new_conv = xbc_tail.reshape(bsz, tail, SSD_CONV_DIM)[:, tail - (SSD_CONV - 1):]
    new_ssm = h_new.reshape(bsz, SSD_HEADS, SSD_HEAD_DIM, SSD_STATE)

    k_new = k_rows.reshape(bsz, length, SB_HEADS, SB_HEAD_DIM)
    v_new = v_rows.reshape(bsz, length, SB_HEADS, SB_HEAD_DIM)
    q3 = q.reshape(bsz, length, SB_WIDTH)
    kvb3 = kvb.reshape(bsz, length, 2 * SB_WIDTH)
    if k_cache is None:
        attn = _sb_prompt(q3, kvb3, tri_ext)
    else:
        depth, _, past = k_cache.shape[:3]
        kvb_pad = jnp.pad(kvb3, ((0, 0), (0, SB_BLOCK - length), (0, 0)))
        attn = _sb_sample(q3, kvb_pad, k_cache.reshape(depth * bsz, past * SB_HEADS, SB_HEAD_DIM),
                          v_cache.reshape(depth * bsz, past * SB_HEADS, SB_HEAD_DIM), tri_ext, layer)

    x2 = _merge(y.reshape(m, D_SSM), zg, attn.reshape(m, SB_WIDTH), x2, lw['ssd_norm'][None, :],
                lw['w_br_ssd'].astype(BF16), lw['w_br_sb'].astype(BF16), lw['w_out'].astype(BF16),
                lw['norm_post_mix'][None, :])

    x3, ffn8 = _ffn(x2.reshape(bsz, length, D_MODEL), _pad_rows_to8(ffn_st), lw['norm_pre_ffn'][None, :],
                    lw['w_up'].astype(BF16), lw['ffn_conv_w'], lw['ffn_conv_b'][None, :],
                    lw['w_down'].astype(BF16), lw['norm_post_ffn'][None, :],
                    nb=max(1, min(bsz, FFN_ROWS // length)), tl=min(FFN_ROWS, length), fc=512)
    new_ffn = ffn8[:, SUBLANES - (FFN_CONV - 1):]

    x2 = _ple(x3.reshape(m, D_MODEL), p.reshape(m, PLE_DIM), lw['w_ple'].astype(BF16),
              lw['w_ple_gate'].astype(BF16), lw['norm_ple'][None, :])
    return x2.reshape(bsz, length, D_MODEL), (new_conv, new_ssm, k_new, v_new, new_ffn)


_LAYER_WEIGHTS = ('norm_pre_mix', 'w_in', 'ssd_conv_w', 'ssd_conv_b', 'ssd_dt_bias', 'ssd_a_log', 'ssd_d',
                  'ssd_norm', 'w_br_ssd', 'w_br_sb', 'w_out', 'norm_post_mix', 'norm_pre_ffn', 'w_up',
                  'ffn_conv_w', 'ffn_conv_b', 'w_down', 'norm_post_ffn', 'w_ple', 'w_ple_gate', 'norm_ple')


def _run(x, p, conv0, ssm0, k0, v0, ffn0, weights):
    per_layer = []
    for i in range(DEPTH):
        lw = {name: weights[name][i] for name in _LAYER_WEIGHTS}
        x, st = _layer(x, p[i], conv0[i], ssm0[i], k0, v0, ffn0[i], lw, i)
        per_layer.append(st)
    return x, [jnp.stack([st[j] for st in per_layer]) for j in range(5)]


def kernel(x_prompt, x_sample, state_ssd_conv, state_ssd, cache_sb_k, cache_sb_v, state_ffn_conv, p_prompt, p_sample, norm_pre_mix, w_in, ssd_conv_w, ssd_conv_b, ssd_dt_bias, ssd_a_log, ssd_d, ssd_norm, w_br_ssd, w_br_sb, w_out, norm_post_mix, norm_pre_ffn, w_up, ffn_conv_w, ffn_conv_b, w_down, norm_post_ffn, w_ple, w_ple_gate, norm_ple):
    weights = dict(norm_pre_mix=norm_pre_mix, w_in=w_in, ssd_conv_w=ssd_conv_w, ssd_conv_b=ssd_conv_b,
                   ssd_dt_bias=ssd_dt_bias, ssd_a_log=ssd_a_log, ssd_d=ssd_d, ssd_norm=ssd_norm,
                   w_br_ssd=w_br_ssd, w_br_sb=w_br_sb, w_out=w_out, norm_post_mix=norm_post_mix,
                   norm_pre_ffn=norm_pre_ffn, w_up=w_up, ffn_conv_w=ffn_conv_w, ffn_conv_b=ffn_conv_b,
                   w_down=w_down, norm_post_ffn=norm_post_ffn, w_ple=w_ple, w_ple_gate=w_ple_gate,
                   norm_ple=norm_ple)
    bp = x_prompt.shape[0]
    zero_conv = jnp.zeros((DEPTH, bp, SSD_CONV - 1, SSD_CONV_DIM), F32)
    zero_ssm = jnp.zeros((DEPTH, bp, SSD_HEADS, SSD_HEAD_DIM, SSD_STATE), F32)
    zero_ffn = jnp.zeros((DEPTH, bp, FFN_CONV - 1, D_FF), F32)
    y_prompt, ps = _run(x_prompt, p_prompt, zero_conv, zero_ssm, None, None, zero_ffn, weights)
    y_sample, ss = _run(x_sample, p_sample, state_ssd_conv, state_ssd, cache_sb_k, cache_sb_v,
                        state_ffn_conv, weights)
    return (y_prompt, y_sample, ps[0], ps[1], ps[2], ps[3], ps[4], ss[0], ss[1], ss[2], ss[3], ss[4])
```

```python
import functools

import jax
import jax.numpy as jnp
from jax import lax
from jax.experimental import pallas as pl
from jax.experimental.pallas import tpu as pltpu

F32 = jnp.float32
BF16 = jnp.bfloat16

D_MODEL = 1024
DEPTH = 2
CHUNK = 64
D_SSM = 2048
SSD_HEAD_DIM = 64
SSD_HEADS = 32
SSD_GROUPS = 4
HEADS_PER_GROUP = 8
SSD_STATE = 128
SSD_CONV = 4
SSD_CONV_DIM = D_SSM + 2 * SSD_GROUPS * SSD_STATE
GROUP_WIDTH = D_SSM // SSD_GROUPS
SB_HEAD_DIM = 128
SB_HEADS = 8
SB_WIDTH = 1024
SB_BLOCK = 128
D_FF = 4096
FFN_CONV = 3
PLE_DIM = 256
EPS = 1e-6
GELU_C = 0.7978845608028654
GELU_A = 0.044715
SB_SCALE = SB_HEAD_DIM ** -0.5
SB_HEAD_GROUP = 8
SB_DEAD_LOG = 120.0

LANES = 128
SUBLANES = 8
VMEM_LIMIT = 56 * 1024 * 1024
FFN_ROWS = 1024


def _cparams(sem):
    return pltpu.CompilerParams(dimension_semantics=sem, vmem_limit_bytes=VMEM_LIMIT)


def _rms(x, gain):
    return x * lax.rsqrt(jnp.mean(x * x, axis=-1, keepdims=True) + EPS) * gain


def _sigmoid(x):
    return 0.5 + 0.5 * jnp.tanh(0.5 * x)


def _silu(x):
    h = 0.5 * x
    return h + h * jnp.tanh(h)


def _dot(a, b):
    return jnp.dot(a, b, preferred_element_type=F32)


def _dot_nt(a, b):
    return lax.dot_general(a, b, (((1,), (1,)), ((), ())), preferred_element_type=F32)


def _dot_tn(a, b):
    return lax.dot_general(a, b, (((0,), (0,)), ((), ())), preferred_element_type=F32)


def _split_bf16(v, n):
    parts = []
    r = v
    for _ in range(n):
        p = r.astype(BF16)
        parts.append(p)
        r = r - p.astype(F32)
    return parts


def _shifted_rows(x, hist, k):
    nseq = hist.shape[0]
    tl = x.shape[0] // nseq
    rolled = pltpu.roll(x, k, 0)
    row = lax.broadcasted_iota(jnp.int32, (SUBLANES, x.shape[1]), 0)
    pieces = []
    for b in range(nseq):
        start = b * tl
        pieces.append(jnp.where(row < k, pltpu.roll(hist[b], k, 0), rolled[start:start + SUBLANES]))
        if tl > SUBLANES:
            pieces.append(rolled[start + SUBLANES:start + tl])
    return pieces[0] if len(pieces) == 1 else jnp.concatenate(pieces, axis=0)


def _causal_conv(x, hist, w, b):
    width = w.shape[0]
    acc = b
    for tap in range(width):
        k = width - 1 - tap
        sh = x if k == 0 else _shifted_rows(x, hist, k)
        acc = acc + sh * w[tap:tap + 1, :]
    return acc


def _norm_matmul_kernel(x_ref, g_ref, w_ref, o_ref, h_ref, *, normalize):
    @pl.when(pl.program_id(1) == 0)
    def _():
        x = x_ref[...]
        if normalize:
            x = _rms(x, g_ref[...])
        h_ref[...] = x.astype(BF16)

    o_ref[...] = _dot(h_ref[...], w_ref[...])


def _norm_matmul(x, gain, w, *, normalize=True):
    m, k = x.shape
    n = w.shape[1]
    tm = min(1024, m)
    tn = min(1024, n)
    return pl.pallas_call(
        functools.partial(_norm_matmul_kernel, normalize=normalize),
        out_shape=jax.ShapeDtypeStruct((m, n), F32),
        grid=(m // tm, n // tn),
        in_specs=[pl.BlockSpec((tm, k), lambda i, j: (i, 0)),
                  pl.BlockSpec((1, k), lambda i, j: (0, 0)),
                  pl.BlockSpec((k, tn), lambda i, j: (0, j))],
        out_specs=pl.BlockSpec((tm, tn), lambda i, j: (i, j)),
        scratch_shapes=[pltpu.VMEM((tm, k), BF16)],
        compiler_params=_cparams(("parallel", "arbitrary")),
        name="norm_matmul",
    )(x, gain, w)


PROJ_TN = 1024
_PROJ_WIDTHS = (2 * D_SSM, SSD_CONV_DIM, SB_WIDTH, SB_WIDTH, SB_WIDTH)
_PROJ_FIRST = tuple(sum(_PROJ_WIDTHS[:n]) // PROJ_TN for n in range(len(_PROJ_WIDTHS) + 1))


def _in_proj_kernel(x_ref, g_ref, w_ref, wdt_ref, zg_ref, xbc_ref, q_ref, k_ref, v_ref, kvb_ref, dt_ref, h_ref):
    j = pl.program_id(1)

    @pl.when(j == 0)
    def _():
        h = _rms(x_ref[...], g_ref[...]).astype(BF16)
        h_ref[...] = h
        dt_ref[...] = _dot(h, wdt_ref[...])

    def owns(n):
        return jnp.logical_and(j >= _PROJ_FIRST[n], j < _PROJ_FIRST[n + 1])

    def proj():
        return _dot(h_ref[...], w_ref[...])

    @pl.when(owns(0))
    def _():
        zg_ref[...] = proj().astype(BF16)

    @pl.when(owns(1))
    def _():
        xbc_ref[...] = proj().astype(BF16)

    @pl.when(owns(2))
    def _():
        q_ref[...] = proj().astype(BF16)

    heads_per_tile = PROJ_TN // SB_HEAD_DIM
    tm = x_ref.shape[0]
    for n, rows_ref in ((3, k_ref), (4, v_ref)):
        for t in range(_PROJ_FIRST[n + 1] - _PROJ_FIRST[n]):
            @pl.when(j == _PROJ_FIRST[n] + t)
            def _(rows_ref=rows_ref, t=t):
                r = proj()
                kvb_ref[...] = r.astype(BF16)
                for hh in range(heads_per_tile):
                    head = t * heads_per_tile + hh
                    rows_ref[pl.ds(head, tm, stride=SB_HEADS), :] = r[:, hh * SB_HEAD_DIM:(hh + 1) * SB_HEAD_DIM]


def _in_proj(x, gain, w, w_dt):
    m, k = x.shape
    tm = min(1024, m)
    tn = PROJ_TN
    bf16 = lambda n: jax.ShapeDtypeStruct((m, n), BF16)
    head_rows = jax.ShapeDtypeStruct((m * SB_HEADS, SB_HEAD_DIM), F32)

    def tile_of(n_first, n_last):
        first, count = _PROJ_FIRST[n_first], _PROJ_FIRST[n_last + 1] - _PROJ_FIRST[n_first]
        return pl.BlockSpec((tm, tn), lambda i, j: (i, jnp.clip(j - first, 0, count - 1)))

    rows_spec = pl.BlockSpec((tm * SB_HEADS, SB_HEAD_DIM), lambda i, j: (i, 0))
    return pl.pallas_call(
        _in_proj_kernel,
        out_shape=(bf16(2 * D_SSM), bf16(SSD_CONV_DIM), bf16(SB_WIDTH), head_rows, head_rows, bf16(2 * SB_WIDTH),
                   jax.ShapeDtypeStruct((m, LANES), F32)),
        grid=(m // tm, _PROJ_FIRST[-1]),
        in_specs=[pl.BlockSpec((tm, k), lambda i, j: (i, 0)),
                  pl.BlockSpec((1, k), lambda i, j: (0, 0)),
                  pl.BlockSpec((k, tn), lambda i, j: (0, j)),
                  pl.BlockSpec((k, LANES), lambda i, j: (0, 0))],
        out_specs=(tile_of(0, 0), tile_of(1, 1), tile_of(2, 2), rows_spec, rows_spec, tile_of(3, 4),
                   pl.BlockSpec((tm, LANES), lambda i, j: (i, 0))),
        scratch_shapes=[pltpu.VMEM((tm, k), BF16)],
        compiler_params=_cparams(("parallel", "arbitrary")),
        name="in_proj",
    )(x, gain, w, w_dt)


def _ssd_kernel(xbc_ref, dt_ref, cst_ref, cw_ref, cb_ref, dtb_ref, alog_ref, dexp_ref, h0_ref, e_ref, tri_ref,
                shift_ref, y_ref, hout_ref, h_scr, carry_scr, xc_scr, dt_scr, *, cl, nchunk):
    l = pl.program_id(1)

    @pl.when(l == 0)
    def _():
        h_scr[...] = h0_ref[0]
        carry_scr[...] = cst_ref[0]

    xb = xbc_ref[0]
    xt = xb.astype(F32)
    tl = xt.shape[0]
    moved = _dot(shift_ref[...], xb)
    hist = carry_scr[...]
    row = lax.broadcasted_iota(jnp.int32, (SUBLANES, SSD_CONV_DIM), 0)
    conv = cb_ref[...] + xt * cw_ref[SSD_CONV - 1:SSD_CONV, :]
    head = jnp.zeros((SUBLANES, SSD_CONV_DIM), F32)
    for back in range(1, SSD_CONV):
        w_tap = cw_ref[SSD_CONV - 1 - back:SSD_CONV - back, :]
        conv = conv + moved[(back - 1) * tl:back * tl] * w_tap
        head = head + jnp.where(row < back, pltpu.roll(hist, back, 0), 0.0) * w_tap
    carry_scr[...] = xt[tl - SUBLANES:tl]
    xc_scr[...] = _silu(conv)
    xc_scr[0:SUBLANES, :] = _silu(conv[0:SUBLANES] + head)
    dtr = dt_ref[0] + dtb_ref[...]
    dt_scr[...] = jnp.maximum(dtr, 0.0) + jnp.log1p(jnp.exp(-jnp.abs(dtr)))

    a_head = -jnp.exp(alog_ref[...])
    tri = tri_ref[...]
    expand = e_ref[...]
    rowi = lax.broadcasted_iota(jnp.int32, (cl, cl), 0)
    coli = lax.broadcasted_iota(jnp.int32, (cl, cl), 1)
    causal = coli <= rowi
    lane = lax.broadcasted_iota(jnp.int32, (1, LANES), 1)
    half_masks = ((lane < SSD_HEAD_DIM).astype(F32), (lane >= SSD_HEAD_DIM).astype(F32))

    def expand_heads(v):
        return _dot(v.astype(BF16), expand)

    def chunk(r0):
        rows = pl.ds(r0, cl)
        xs = xc_scr[rows, 0:D_SSM]
        bm = xc_scr[rows, D_SSM:D_SSM + SSD_GROUPS * SSD_STATE]
        cm = xc_scr[rows, D_SSM + SSD_GROUPS * SSD_STATE:SSD_CONV_DIM]
        dtc = dt_scr[rows, :]
        a = dtc * a_head
        acum = sum(_dot(tri, p) for p in _split_bf16(a, 3))
        if cl < LANES:
            acum_sq = jnp.concatenate([acum, jnp.zeros((LANES - cl, LANES), F32)], axis=0)
        else:
            acum_sq = acum
        acum_t = acum_sq.T
        a_end = acum[cl - 1:cl, :]
        dt_e = expand_heads(dtc)
        dec_end_e = expand_heads(jnp.exp(a_end - acum))
        dec_in_e = expand_heads(jnp.exp(acum))
        chunk_decay = jnp.broadcast_to(jnp.exp(acum_t[:, cl - 1:cl]), (LANES, LANES))
        x_dt = xs * dt_e
        x_end = x_dt * dec_end_e
        groups = range(SSD_GROUPS)
        heads = range(SSD_HEADS)
        gss = [slice(g * SSD_STATE, (g + 1) * SSD_STATE) for g in groups]
        gws = [slice(g * GROUP_WIDTH, (g + 1) * GROUP_WIDTH) for g in groups]
        bgs = [bm[:, gs].astype(BF16) for gs in gss]
        cgs = [cm[:, gs].astype(BF16) for gs in gss]
        cbs = [_dot_nt(cg, bg) for cg, bg in zip(cgs, bgs)]
        y_offs = [_dot_nt(cg, h_scr[gw, :].astype(BF16)) for cg, gw in zip(cgs, gws)]
        sts = [_dot_tn(x_end[:, gw].astype(BF16), bg) for gw, bg in zip(gws, bgs)]
        m_es = []
        for e in heads:
            seg = acum[:, e:e + 1] - acum_t[e:e + 1, 0:cl]
            dec = jnp.exp(jnp.where(causal, seg, -jnp.inf))
            m_es.append((cbs[e // HEADS_PER_GROUP] * dec).astype(BF16))
        for pr in range(SSD_HEADS // 2):
            g = pr // (HEADS_PER_GROUP // 2)
            ps = slice(pr * LANES, (pr + 1) * LANES)
            pg = slice(pr * LANES - g * GROUP_WIDTH, (pr + 1) * LANES - g * GROUP_WIDTH)
            xp = x_dt[:, ps]
            yp = y_offs[g][:, pg] * dec_in_e[:, ps]
            for sub in range(2):
                yp = yp + _dot(m_es[2 * pr + sub], (xp * half_masks[sub]).astype(BF16))
            y_ref[0, rows, ps] = yp + dexp_ref[:, ps] * xs[:, ps]
        for e in heads:
            g = e // HEADS_PER_GROUP
            hs = slice(e * SSD_HEAD_DIM, (e + 1) * SSD_HEAD_DIM)
            ls = slice(e * SSD_HEAD_DIM - g * GROUP_WIDTH, (e + 1) * SSD_HEAD_DIM - g * GROUP_WIDTH)
            h_scr[hs, :] = h_scr[hs, :] * chunk_decay[e:e + 1, :] + sts[g][ls, :]

    if nchunk == 1:
        chunk(0)
    else:
        def body(c, carry):
            chunk(pl.multiple_of(c * cl, cl))
            return carry
        lax.fori_loop(0, nchunk, body, 0)

    @pl.when(l == pl.num_programs(1) - 1)
    def _():
        hout_ref[0] = h_scr[...]


def _ssd(xbc, dt_raw, conv_state8, conv_w, conv_b, dt_bias, a_log, d_exp, h0, expand, tri, *, tl, cl):
    bsz, length, _ = xbc.shape
    kernel = functools.partial(_ssd_kernel, cl=cl, nchunk=tl // cl)
    full2 = lambda b, l: (0, 0)
    src = jnp.arange(tl)[None, None, :]
    dst = jnp.arange(tl)[None, :, None]
    back = jnp.arange(1, SSD_CONV)[:, None, None]
    shift = (src == dst - back).astype(BF16).reshape((SSD_CONV - 1) * tl, tl)
    return pl.pallas_call(
        kernel,
        out_shape=(jax.ShapeDtypeStruct((bsz, length, D_SSM), F32),
                   jax.ShapeDtypeStruct((bsz, D_SSM, SSD_STATE), F32)),
        grid=(bsz, length // tl),
        in_specs=[pl.BlockSpec((1, tl, SSD_CONV_DIM), lambda b, l: (b, l, 0)),
                  pl.BlockSpec((1, tl, LANES), lambda b, l: (b, l, 0)),
                  pl.BlockSpec((1, SUBLANES, SSD_CONV_DIM), lambda b, l: (b, 0, 0)),
                  pl.BlockSpec((SSD_CONV, SSD_CONV_DIM), full2),
                  pl.BlockSpec((1, SSD_CONV_DIM), full2),
                  pl.BlockSpec((1, LANES), full2),
                  pl.BlockSpec((1, LANES), full2),
                  pl.BlockSpec((1, D_SSM), full2),
                  pl.BlockSpec((1, D_SSM, SSD_STATE), lambda b, l: (b, 0, 0)),
                  pl.BlockSpec((LANES, D_SSM), full2),
                  pl.BlockSpec((cl, cl), full2),
                  pl.BlockSpec(((SSD_CONV - 1) * tl, tl), full2)],
        out_specs=(pl.BlockSpec((1, tl, D_SSM), lambda b, l: (b, l, 0)),
                   pl.BlockSpec((1, D_SSM, SSD_STATE), lambda b, l: (b, 0, 0))),
        scratch_shapes=[pltpu.VMEM((D_SSM, SSD_STATE), F32),
                        pltpu.VMEM((SUBLANES, SSD_CONV_DIM), F32),
                        pltpu.VMEM((tl, SSD_CONV_DIM), F32),
                        pltpu.VMEM((tl, LANES), F32)],
        compiler_params=_cparams(("parallel", "arbitrary")),
        name="ssd",
    )(xbc, dt_raw, conv_state8, conv_w, conv_b, dt_bias, a_log, d_exp, h0, expand, tri, shift)


def _sb_blocks(qs, blocks, laters, accs, tri_ext):
    zs = [[_dot_nt(q, kb) * SB_SCALE for q, kb in zip(qs, kbs)] for kbs, _, _ in blocks]
    log_keeps, log_betas = [], []
    for z_row, (_, _, earlier) in zip(zs, blocks):
        keep_row, beta_row = [], []
        for z in z_row:
            l1p = jnp.log(1.0 + jnp.exp(-jnp.abs(z)))
            log_keep = -(jnp.maximum(z, 0.0) + l1p)
            if earlier is not None:
                log_keep = jnp.where(earlier, log_keep, 0.0)
            keep_row.append(log_keep)
            beta_row.append(jnp.minimum(z, 0.0) - l1p)
        log_keeps.append(keep_row)
        log_betas.append(beta_row)
    exts = [[_dot(lk.astype(BF16), tri_ext) for lk in keep_row] for keep_row in log_keeps]
    ws = []
    for beta_row, ext_row, (_, _, earlier) in zip(log_betas, exts, blocks):
        w_row = []
        for log_beta, ext, later in zip(beta_row, ext_row, laters):
            w = jnp.exp(log_beta + ext[:, :SB_BLOCK] + later)
            if earlier is not None:
                w = jnp.where(earlier, w, 0.0)
            w_row.append(w.astype(BF16))
        ws.append(w_row)
        laters = [later + ext[:, SB_BLOCK:] for later, ext in zip(laters, ext_row)]
    for w_row, (_, vbs, _) in zip(ws, blocks):
        accs = [acc + _dot(w, vb) for acc, w, vb in zip(accs, w_row, vbs)]
    return laters, accs


def _sb_cols(h):
    return slice(h * SB_HEAD_DIM, (h + 1) * SB_HEAD_DIM)


def _sb_alive(later_scr):
    return jnp.max(later_scr[...]) > -SB_DEAD_LOG


def _sb_heads(q_ref, blocks, later_scr, acc_scr, tri_ext, init):
    tq = q_ref.shape[1]
    for h0 in range(0, SB_HEADS, SB_HEAD_GROUP):
        heads = range(h0, h0 + SB_HEAD_GROUP)
        qs = [q_ref[0, :, _sb_cols(h)] for h in heads]
        if init:
            laters = [jnp.zeros((tq, SB_BLOCK), F32) for _ in heads]
            accs = [jnp.zeros((tq, SB_HEAD_DIM), F32) for _ in heads]
        else:
            laters = [later_scr[h] for h in heads]
            accs = [acc_scr[h] for h in heads]
        per_head = [([k_of(h) for h in heads], [v_of(h) for h in heads], earlier)
                    for k_of, v_of, earlier in blocks]
        laters, accs = _sb_blocks(qs, per_head, laters, accs, tri_ext)
        for h, later, acc in zip(heads, laters, accs):
            later_scr[h] = later
            acc_scr[h] = acc


def _sb_kv_of(read):
    return (lambda h: read(_sb_cols(h))), (lambda h: read(_sb_cols(SB_HEADS + h)))


def _sb_diag_mask(tq):
    key = lax.broadcasted_iota(jnp.int32, (tq, SB_BLOCK), 1)
    query = lax.broadcasted_iota(jnp.int32, (tq, SB_BLOCK), 0)
    return key < query


def _sb_prompt_kernel(q_ref, kv0_ref, kv1_ref, kv2_ref, kv_hbm, tri_ref, o_ref,
                      later_scr, acc_scr, kv_buf, sem):
    b = pl.program_id(0)
    qi = pl.program_id(1)
    tri_ext = tri_ref[...]
    state = (later_scr, acc_scr, tri_ext)

    k0, v0 = _sb_kv_of(lambda c: kv0_ref[0, :, c])
    diagonal = (k0, v0, _sb_diag_mask(q_ref.shape[1]))

    @pl.when(qi >= 2)
    def _():
        k1, v1 = _sb_kv_of(lambda c: kv1_ref[0, :, c])
        k2, v2 = _sb_kv_of(lambda c: kv2_ref[0, :, c])
        _sb_heads(q_ref, [diagonal, (k1, v1, None), (k2, v2, None)], *state, init=True)

    @pl.when(qi < 2)
    def _():
        _sb_heads(q_ref, [diagonal], *state, init=True)

    def more(carry):
        j, alive = carry
        return jnp.logical_and(j >= 0, alive)

    def body(carry):
        j, _ = carry
        rows = pl.ds(pl.multiple_of(j * SB_BLOCK, SB_BLOCK), SB_BLOCK)
        copy = pltpu.make_async_copy(kv_hbm.at[b, rows, :], kv_buf, sem)
        copy.start()
        copy.wait()
        kb, vb = _sb_kv_of(lambda c: kv_buf[:, c])
        _sb_heads(q_ref, [(kb, vb, None)], *state, init=False)
        return j - 1, _sb_alive(later_scr)

    lax.while_loop(more, body, (jnp.where(qi >= 2, qi - 3, qi - 1), _sb_alive(later_scr)))
    for h in range(SB_HEADS):
        o_ref[0, :, _sb_cols(h)] = acc_scr[h].astype(o_ref.dtype)


def _sb_prompt(q, kvb, tri_ext):
    bsz, length, _ = q.shape
    tq = SB_BLOCK
    blk = lambda back: pl.BlockSpec((1, SB_BLOCK, 2 * SB_WIDTH), lambda b, i: (b, jnp.maximum(i - back, 0), 0))
    return pl.pallas_call(
        _sb_prompt_kernel,
        out_shape=jax.ShapeDtypeStruct((bsz, length, SB_WIDTH), BF16),
        grid=(bsz, length // tq),
        in_specs=[pl.BlockSpec((1, tq, SB_WIDTH), lambda b, i: (b, i, 0)),
                  blk(0), blk(1), blk(2),
                  pl.BlockSpec(memory_space=pl.ANY),
                  pl.BlockSpec((SB_BLOCK, 2 * SB_BLOCK), lambda b, i: (0, 0))],
        out_specs=pl.BlockSpec((1, tq, SB_WIDTH), lambda b, i: (b, i, 0)),
        scratch_shapes=[pltpu.VMEM((SB_HEADS, tq, SB_BLOCK), F32),
                        pltpu.VMEM((SB_HEADS, tq, SB_HEAD_DIM), F32),
                        pltpu.VMEM((SB_BLOCK, 2 * SB_WIDTH), BF16),
                        pltpu.SemaphoreType.DMA(())],
        compiler_params=_cparams(("parallel", "arbitrary")),
        name="sb_prompt",
    )(q, kvb, kvb, kvb, kvb, tri_ext)


SB_BLOCK_ROWS = SB_BLOCK * SB_HEADS
SB_RECENT_BLOCKS = 2


def _sb_cache_heads(read):
    return lambda first_row: (lambda h: read(pl.ds(first_row + h, SB_BLOCK, stride=SB_HEADS)).astype(BF16))


def _sb_sample_kernel(q_ref, kvn_ref, kr_ref, vr_ref, k_hbm, v_hbm, tri_ref, o_ref,
                      later_scr, acc_scr, k_buf, v_buf, sem, *, npast, first):
    b = pl.program_id(0)
    tri_ext = tri_ref[...]
    state = (later_scr, acc_scr, tri_ext)
    kn, vn = _sb_kv_of(lambda c: kvn_ref[0, :, c])
    k_recent = _sb_cache_heads(lambda rows: kr_ref[0, rows, :])
    v_recent = _sb_cache_heads(lambda rows: vr_ref[0, rows, :])
    recent = [(k_recent(t * SB_BLOCK_ROWS), v_recent(t * SB_BLOCK_ROWS), None)
              for t in reversed(range(SB_RECENT_BLOCKS))]
    _sb_heads(q_ref, [(kn, vn, _sb_diag_mask(q_ref.shape[1]))] + recent, *state, init=True)

    def more(carry):
        j, alive = carry
        return jnp.logical_and(j >= 0, alive)

    def body(carry):
        j, _ = carry
        rows = pl.ds(pl.multiple_of(j * SB_BLOCK_ROWS, SB_BLOCK_ROWS), SB_BLOCK_ROWS)
        copies = [pltpu.make_async_copy(hbm.at[first + b, rows, :], buf, sem.at[n])
                  for n, (hbm, buf) in enumerate(((k_hbm, k_buf), (v_hbm, v_buf)))]
        for copy in copies:
            copy.start()
        for copy in copies:
            copy.wait()
        kb = _sb_cache_heads(lambda r: k_buf[r, :])(0)
        vb = _sb_cache_heads(lambda r: v_buf[r, :])(0)
        _sb_heads(q_ref, [(kb, vb, None)], *state, init=False)
        return j - 1, _sb_alive(later_scr)

    lax.while_loop(more, body, (npast - SB_RECENT_BLOCKS - 1, _sb_alive(later_scr)))
    for h in range(SB_HEADS):
        o_ref[0, :, _sb_cols(h)] = acc_scr[h].astype(o_ref.dtype)


def _sb_sample(q, kvb_new_pad, k_past, v_past, tri_ext, layer):
    bsz, lq, _ = q.shape
    rows = k_past.shape[1]
    npast = rows // SB_BLOCK_ROWS
    assert rows % SB_BLOCK_ROWS == 0 and npast >= SB_RECENT_BLOCKS and lq <= SB_BLOCK
    first = layer * bsz
    recent_rows = SB_RECENT_BLOCKS * SB_BLOCK_ROWS
    assert rows % recent_rows == 0
    recent = pl.BlockSpec((1, recent_rows, SB_HEAD_DIM), lambda b: (first + b, rows // recent_rows - 1, 0))
    return pl.pallas_call(
        functools.partial(_sb_sample_kernel, npast=npast, first=first),
        out_shape=jax.ShapeDtypeStruct((bsz, lq, SB_WIDTH), BF16),
        grid=(bsz,),
        in_specs=[pl.BlockSpec((1, lq, SB_WIDTH), lambda b: (b, 0, 0)),
                  pl.BlockSpec((1, SB_BLOCK, 2 * SB_WIDTH), lambda b: (b, 0, 0)),
                  recent, recent,
                  pl.BlockSpec(memory_space=pl.ANY),
                  pl.BlockSpec(memory_space=pl.ANY),
                  pl.BlockSpec((SB_BLOCK, 2 * SB_BLOCK), lambda b: (0, 0))],
        out_specs=pl.BlockSpec((1, lq, SB_WIDTH), lambda b: (b, 0, 0)),
        scratch_shapes=[pltpu.VMEM((SB_HEADS, lq, SB_BLOCK), F32),
                        pltpu.VMEM((SB_HEADS, lq, SB_HEAD_DIM), F32),
                        pltpu.VMEM((SB_BLOCK_ROWS, SB_HEAD_DIM), F32),
                        pltpu.VMEM((SB_BLOCK_ROWS, SB_HEAD_DIM), F32),
                        pltpu.SemaphoreType.DMA((2,))],
        compiler_params=_cparams(("parallel",)),
        name="sb_sample",
    )(q, kvb_new_pad, k_past, v_past, k_past, v_past, tri_ext)


def _merge_kernel(y_ref, z_ref, gate_ref, attn_ref, x_ref, nw_ref, w1_ref, w2_ref, w3_ref, gpost_ref, o_ref):
    z = z_ref[...].astype(F32)
    g = y_ref[...] * _silu(z)
    b1 = None
    for k in range(SSD_GROUPS):
        gw = slice(k * GROUP_WIDTH, (k + 1) * GROUP_WIDTH)
        gk = _rms(g[:, gw], nw_ref[:, gw])
        part = _dot(gk.astype(BF16), w1_ref[gw, :])
        b1 = part if b1 is None else b1 + part
    b2 = _dot(attn_ref[...], w2_ref[...])
    gl = gate_ref[...].astype(F32)
    m = _sigmoid(gl[:, :D_MODEL]) * b1 + _sigmoid(gl[:, D_MODEL:]) * b2
    mixed = _dot(m.astype(BF16), w3_ref[...])
    o_ref[...] = x_ref[...] + _rms(mixed, gpost_ref[...])


def _merge(y, zg, attn, x, ssd_norm, w_br_ssd, w_br_sb, w_out, norm_post):
    m = x.shape[0]
    tm = min(256, m)
    row = lambda i: (i, 0)
    full = lambda i: (0, 0)
    return pl.pallas_call(
        _merge_kernel,
        out_shape=jax.ShapeDtypeStruct((m, D_MODEL), F32),
        grid=(m // tm,),
        in_specs=[pl.BlockSpec((tm, D_SSM), row),
                  pl.BlockSpec((tm, D_SSM), row),
                  pl.BlockSpec((tm, 2 * D_MODEL), lambda i: (i, 1)),
                  pl.BlockSpec((tm, SB_WIDTH), row),
                  pl.BlockSpec((tm, D_MODEL), row),
                  pl.BlockSpec((1, D_SSM), full),
                  pl.BlockSpec((D_SSM, D_MODEL), full),
                  pl.BlockSpec((SB_WIDTH, D_MODEL), full),
                  pl.BlockSpec((D_MODEL, D_MODEL), full),
                  pl.BlockSpec((1, D_MODEL), full)],
        out_specs=pl.BlockSpec((tm, D_MODEL), row),
        compiler_params=_cparams(("parallel",)),
        name="merge",
    )(y, zg, zg, attn, x, ssd_norm, w_br_ssd, w_br_sb, w_out, norm_post)


def _ffn_kernel(x_ref, st_ref, gpre_ref, wg_ref, wu_ref, cw_ref, cb_ref, wd_ref, gpost_ref,
                p_ref, wple_ref, wpg_ref, gple_ref, o_ref, nst_ref, h_scr, acc_scr, carry_scr):
    l = pl.program_id(1)
    j = pl.program_id(2)
    nb, tl, _ = x_ref.shape

    @pl.when(j == 0)
    def _():
        h_scr[...] = _rms(x_ref[...].reshape(nb * tl, D_MODEL), gpre_ref[...]).astype(BF16)
        acc_scr[...] = jnp.zeros_like(acc_scr)

    @pl.when(l == 0)
    def _():
        carry_scr[j] = st_ref[...]

    h = h_scr[...]
    gate = _dot(h, wg_ref[...])
    up = _dot(h, wu_ref[...])
    conv = _causal_conv(gate, carry_scr[j], cw_ref[...], cb_ref[...])
    tail = gate.reshape(nb, tl, gate.shape[1])[:, tl - SUBLANES:, :]
    carry_scr[j] = tail

    @pl.when(l == pl.num_programs(1) - 1)
    def _():
        nst_ref[...] = tail

    half = 0.5 * conv
    gelu = half + half * jnp.tanh(conv * (GELU_C + (GELU_C * GELU_A) * (conv * conv)))
    acc_scr[...] += _dot((gelu * up).astype(BF16), wd_ref[...])

    @pl.when(j == pl.num_programs(2) - 1)
    def _():
        x1 = x_ref[...].reshape(nb * tl, D_MODEL) + _rms(acc_scr[...], gpost_ref[...])
        emb = _dot(p_ref[...].reshape(nb * tl, PLE_DIM).astype(BF16), wple_ref[...])
        ple = emb * _sigmoid(_dot(x1.astype(BF16), wpg_ref[...]))
        o_ref[...] = (x1 + _rms(ple, gple_ref[...])).reshape(nb, tl, D_MODEL)


def _ffn(x, state8, norm_pre, w_up, conv_w, conv_b, w_down, norm_post, p, w_ple, w_ple_gate, norm_ple,
         *, nb, tl, fc):
    bsz, length, _ = x.shape
    nf = D_FF // fc
    nl = length // tl
    full = lambda b, l, j: (0, 0)
    nst_map = lambda b, l, j: (b, 0, jnp.where(l == nl - 1, j, 0))
    return pl.pallas_call(
        _ffn_kernel,
        out_shape=(jax.ShapeDtypeStruct((bsz, length, D_MODEL), F32),
                   jax.ShapeDtypeStruct((bsz, SUBLANES, D_FF), F32)),
        grid=(bsz // nb, nl, nf),
        in_specs=[pl.BlockSpec((nb, tl, D_MODEL), lambda b, l, j: (b, l, 0)),
                  pl.BlockSpec((nb, SUBLANES, fc), lambda b, l, j: (b, 0, j)),
                  pl.BlockSpec((1, D_MODEL), full),
                  pl.BlockSpec((D_MODEL, fc), lambda b, l, j: (0, j)),
                  pl.BlockSpec((D_MODEL, fc), lambda b, l, j: (0, nf + j)),
                  pl.BlockSpec((FFN_CONV, fc), lambda b, l, j: (0, j)),
                  pl.BlockSpec((1, fc), lambda b, l, j: (0, j)),
                  pl.BlockSpec((fc, D_MODEL), lambda b, l, j: (j, 0)),
                  pl.BlockSpec((1, D_MODEL), full),
                  pl.BlockSpec((nb, tl, PLE_DIM), lambda b, l, j: (b, l, 0)),
                  pl.BlockSpec((PLE_DIM, D_MODEL), full),
                  pl.BlockSpec((D_MODEL, D_MODEL), full),
                  pl.BlockSpec((1, D_MODEL), full)],
        out_specs=(pl.BlockSpec((nb, tl, D_MODEL), lambda b, l, j: (b, l, 0)),
                   pl.BlockSpec((nb, SUBLANES, fc), nst_map)),
        scratch_shapes=[pltpu.VMEM((nb * tl, D_MODEL), BF16),
                        pltpu.VMEM((nb * tl, D_MODEL), F32),
                        pltpu.VMEM((nf, nb, SUBLANES, fc), F32)],
        compiler_params=_cparams(("parallel", "arbitrary", "arbitrary")),
        name="ffn",
    )(x, state8, norm_pre, w_up, w_up, conv_w, conv_b, w_down, norm_post, p, w_ple, w_ple_gate, norm_ple)


def _pad_rows_to8(state):
    return jnp.pad(state, ((0, 0), (SUBLANES - state.shape[1], 0), (0, 0)))


def _pad_lanes(v):
    return jnp.pad(v, (0, LANES - v.shape[0]))[None, :]


def _constants(cl):
    head = jnp.arange(LANES)[:, None]
    chan = jnp.arange(D_SSM)[None, :] // SSD_HEAD_DIM
    expand = (head == chan).astype(BF16)
    idx = jnp.arange(cl)
    tri_cum = (idx[None, :] <= idx[:, None]).astype(BF16)
    kidx = jnp.arange(SB_BLOCK)
    tri_in = (kidx[:, None] > kidx[None, :]).astype(BF16)
    tri_ext = jnp.concatenate([tri_in, jnp.ones((SB_BLOCK, SB_BLOCK), BF16)], axis=1)
    return expand, tri_cum, tri_ext


def _layer(x, p, conv_st, ssm_st, k_cache, v_cache, ffn_st, lw, layer):
    bsz, length, _ = x.shape
    m = bsz * length
    x2 = x.reshape(m, D_MODEL)
    cl = CHUNK if length >= CHUNK else length
    expand, tri_cum, tri_ext = _constants(cl)

    w_in = lw['w_in']
    o_z, o_xbc, o_dt, o_q, o_g = 0, D_SSM, D_SSM + SSD_CONV_DIM, D_SSM + SSD_CONV_DIM + SSD_HEADS, \
        D_SSM + SSD_CONV_DIM + SSD_HEADS + 3 * SB_WIDTH
    w_main = jnp.concatenate([w_in[:, o_z:o_xbc], w_in[:, o_g:], w_in[:, o_xbc:o_dt], w_in[:, o_q:o_g]],
                             axis=1).astype(BF16)
    w_dt = jnp.pad(w_in[:, o_dt:o_q], ((0, 0), (0, LANES - SSD_HEADS))).astype(BF16)
    g_pre = lw['norm_pre_mix'][None, :]
    zg, xbc, q, k_rows, v_rows, kvb, dt_raw = _in_proj(x2, g_pre, w_main, w_dt)

    xbc3 = xbc.reshape(bsz, length, SSD_CONV_DIM)
    tl = min(256, length)
    y, h_new = _ssd(xbc3, dt_raw.reshape(bsz, length, LANES), _pad_rows_to8(conv_st),
                    lw['ssd_conv_w'], lw['ssd_conv_b'][None, :], _pad_lanes(lw['ssd_dt_bias']),
                    _pad_lanes(lw['ssd_a_log']), jnp.repeat(lw['ssd_d'], SSD_HEAD_DIM)[None, :],
                    ssm_st.reshape(bsz, D_SSM, SSD_STATE), expand, tri_cum, tl=tl, cl=cl)
    tail = 2 * SUBLANES
    x_tail = x[:, length - tail:, :].reshape(bsz * tail, D_MODEL)
    xbc_tail = _norm_matmul(x_tail, g_pre, w_in[:, o_xbc:o_dt].astype(BF16))
    new_conv = xbc_tail.reshape(bsz, tail, SSD_CONV_DIM)[:, tail - (SSD_CONV - 1):]
    new_ssm = h_new.reshape(bsz, SSD_HEADS, SSD_HEAD_DIM, SSD_STATE)

    k_new = k_rows.reshape(bsz, length, SB_HEADS, SB_HEAD_DIM)
    v_new = v_rows.reshape(bsz, length, SB_HEADS, SB_HEAD_DIM)
    q3 = q.reshape(bsz, length, SB_WIDTH)
    kvb3 = kvb.reshape(bsz, length, 2 * SB_WIDTH)
    if k_cache is None:
        attn = _sb_prompt(q3, kvb3, tri_ext)
    else:
        depth, _, past = k_cache.shape[:3]
        kvb_pad = jnp.pad(kvb3, ((0, 0), (0, SB_BLOCK - length), (0, 0)))
        attn = _sb_sample(q3, kvb_pad, k_cache.reshape(depth * bsz, past * SB_HEADS, SB_HEAD_DIM),
                          v_cache.reshape(depth * bsz, past * SB_HEADS, SB_HEAD_DIM), tri_ext, layer)

    x2 = _merge(y.reshape(m, D_SSM), zg, attn.reshape(m, SB_WIDTH), x2, lw['ssd_norm'][None, :],
                lw['w_br_ssd'].astype(BF16), lw['w_br_sb'].astype(BF16), lw['w_out'].astype(BF16),
                lw['norm_post_mix'][None, :])

    x3, ffn8 = _ffn(x2.reshape(bsz, length, D_MODEL), _pad_rows_to8(ffn_st), lw['norm_pre_ffn'][None, :],
                    lw['w_up'].astype(BF16), lw['ffn_conv_w'], lw['ffn_conv_b'][None, :],
                    lw['w_down'].astype(BF16), lw['norm_post_ffn'][None, :],
                    p, lw['w_ple'].astype(BF16), lw['w_ple_gate'].astype(BF16), lw['norm_ple'][None, :],
                    nb=max(1, min(bsz, FFN_ROWS // length)), tl=min(FFN_ROWS, length), fc=512)
    new_ffn = ffn8[:, SUBLANES - (FFN_CONV - 1):]
    return x3, (new_conv, new_ssm, k_new, v_new, new_ffn)


_LAYER_WEIGHTS = ('norm_pre_mix', 'w_in', 'ssd_conv_w', 'ssd_conv_b', 'ssd_dt_bias', 'ssd_a_log', 'ssd_d',
                  'ssd_norm', 'w_br_ssd', 'w_br_sb', 'w_out', 'norm_post_mix', 'norm_pre_ffn', 'w_up',
                  'ffn_conv_w', 'ffn_conv_b', 'w_down', 'norm_post_ffn', 'w_ple', 'w_ple_gate', 'norm_ple')


def _run(x, p, conv0, ssm0, k0, v0, ffn0, weights):
    per_layer = []
    for i in range(DEPTH):
        lw = {name: weights[name][i] for name in _LAYER_WEIGHTS}
        x, st = _layer(x, p[i], conv0[i], ssm0[i], k0, v0, ffn0[i], lw, i)
        per_layer.append(st)
    return x, [jnp.stack([st[j] for st in per_layer]) for j in range(5)]


def kernel(x_prompt, x_sample, state_ssd_conv, state_ssd, cache_sb_k, cache_sb_v, state_ffn_conv, p_prompt, p_sample, norm_pre_mix, w_in, ssd_conv_w, ssd_conv_b, ssd_dt_bias, ssd_a_log, ssd_d, ssd_norm, w_br_ssd, w_br_sb, w_out, norm_post_mix, norm_pre_ffn, w_up, ffn_conv_w, ffn_conv_b, w_down, norm_post_ffn, w_ple, w_ple_gate, norm_ple):
    weights = dict(norm_pre_mix=norm_pre_mix, w_in=w_in, ssd_conv_w=ssd_conv_w, ssd_conv_b=ssd_conv_b,
                   ssd_dt_bias=ssd_dt_bias, ssd_a_log=ssd_a_log, ssd_d=ssd_d, ssd_norm=ssd_norm,
                   w_br_ssd=w_br_ssd, w_br_sb=w_br_sb, w_out=w_out, norm_post_mix=norm_post_mix,
                   norm_pre_ffn=norm_pre_ffn, w_up=w_up, ffn_conv_w=ffn_conv_w, ffn_conv_b=ffn_conv_b,
                   w_down=w_down, norm_post_ffn=norm_post_ffn, w_ple=w_ple, w_ple_gate=w_ple_gate,
                   norm_ple=norm_ple)
    bp = x_prompt.shape[0]
    zero_conv = jnp.zeros((DEPTH, bp, SSD_CONV - 1, SSD_CONV_DIM), F32)
    zero_ssm = jnp.zeros((DEPTH, bp, SSD_HEADS, SSD_HEAD_DIM, SSD_STATE), F32)
    zero_ffn = jnp.zeros((DEPTH, bp, FFN_CONV - 1, D_FF), F32)
    y_prompt, ps = _run(x_prompt, p_prompt, zero_conv, zero_ssm, None, None, zero_ffn, weights)
    y_sample, ss = _run(x_sample, p_sample, state_ssd_conv, state_ssd, cache_sb_k, cache_sb_v,
                        state_ffn_conv, weights)
    return (y_prompt, y_sample, ps[0], ps[1], ps[2], ps[3], ps[4], ss[0], ss[1], ss[2], ss[3], ss[4])
```

```python
import functools

import jax
import jax.numpy as jnp
from jax import lax
from jax.experimental import pallas as pl
from jax.experimental.pallas import tpu as pltpu

F32 = jnp.float32
BF16 = jnp.bfloat16

D_MODEL = 1024
DEPTH = 2
CHUNK = 64
D_SSM = 2048
SSD_HEAD_DIM = 64
SSD_HEADS = 32
SSD_GROUPS = 4
HEADS_PER_GROUP = 8
SSD_STATE = 128
SSD_CONV = 4
SSD_CONV_DIM = D_SSM + 2 * SSD_GROUPS * SSD_STATE
GROUP_WIDTH = D_SSM // SSD_GROUPS
SB_HEAD_DIM = 128
SB_HEADS = 8
SB_WIDTH = 1024
SB_BLOCK = 128
D_FF = 4096
FFN_CONV = 3
PLE_DIM = 256
EPS = 1e-6
GELU_C = 0.7978845608028654
GELU_A = 0.044715
SB_SCALE = SB_HEAD_DIM ** -0.5
SB_HEAD_GROUP = 8
SB_DEAD_LOG = 120.0

LANES = 128
SUBLANES = 8
VMEM_LIMIT = 56 * 1024 * 1024
FFN_ROWS = 1024


def _cparams(sem):
    return pltpu.CompilerParams(dimension_semantics=sem, vmem_limit_bytes=VMEM_LIMIT)


def _rms(x, gain):
    return x * lax.rsqrt(jnp.mean(x * x, axis=-1, keepdims=True) + EPS) * gain


def _sigmoid(x):
    return 0.5 + 0.5 * jnp.tanh(0.5 * x)


def _silu(x):
    h = 0.5 * x
    return h + h * jnp.tanh(h)


def _dot(a, b):
    return jnp.dot(a, b, preferred_element_type=F32)


def _dot_nt(a, b):
    return lax.dot_general(a, b, (((1,), (1,)), ((), ())), preferred_element_type=F32)


def _dot_tn(a, b):
    return lax.dot_general(a, b, (((0,), (0,)), ((), ())), preferred_element_type=F32)


def _split_bf16(v, n):
    parts = []
    r = v
    for _ in range(n):
        p = r.astype(BF16)
        parts.append(p)
        r = r - p.astype(F32)
    return parts


def _shifted_rows(x, hist, k):
    nseq = hist.shape[0]
    tl = x.shape[0] // nseq
    rolled = pltpu.roll(x, k, 0)
    row = lax.broadcasted_iota(jnp.int32, (SUBLANES, x.shape[1]), 0)
    pieces = []
    for b in range(nseq):
        start = b * tl
        pieces.append(jnp.where(row < k, pltpu.roll(hist[b], k, 0), rolled[start:start + SUBLANES]))
        if tl > SUBLANES:
            pieces.append(rolled[start + SUBLANES:start + tl])
    return pieces[0] if len(pieces) == 1 else jnp.concatenate(pieces, axis=0)


def _causal_conv(x, hist, w, b):
    width = w.shape[0]
    acc = b
    for tap in range(width):
        k = width - 1 - tap
        sh = x if k == 0 else _shifted_rows(x, hist, k)
        acc = acc + sh * w[tap:tap + 1, :]
    return acc


def _norm_matmul_kernel(x_ref, g_ref, w_ref, o_ref, h_ref):
    @pl.when(pl.program_id(1) == 0)
    def _():
        h_ref[...] = _rms(x_ref[...], g_ref[...]).astype(BF16)

    o_ref[...] = _dot(h_ref[...], w_ref[...])


def _norm_matmul(x, gain, w):
    m, k = x.shape
    n = w.shape[1]
    tm = min(1024, m)
    tn = min(1024, n)
    return pl.pallas_call(
        _norm_matmul_kernel,
        out_shape=jax.ShapeDtypeStruct((m, n), F32),
        grid=(m // tm, n // tn),
        in_specs=[pl.BlockSpec((tm, k), lambda i, j: (i, 0)),
                  pl.BlockSpec((1, k), lambda i, j: (0, 0)),
                  pl.BlockSpec((k, tn), lambda i, j: (0, j))],
        out_specs=pl.BlockSpec((tm, tn), lambda i, j: (i, j)),
        scratch_shapes=[pltpu.VMEM((tm, k), BF16)],
        compiler_params=_cparams(("parallel", "arbitrary")),
        name="norm_matmul",
    )(x, gain, w)


PROJ_TN = 1024
_PROJ_WIDTHS = (2 * D_SSM, SSD_CONV_DIM, SB_WIDTH, SB_WIDTH, SB_WIDTH)
_PROJ_FIRST = tuple(sum(_PROJ_WIDTHS[:n]) // PROJ_TN for n in range(len(_PROJ_WIDTHS) + 1))


def _in_proj_kernel(x_ref, g_ref, w_ref, wdt_ref, *refs, carried):
    zg_ref, xbc_ref, q_ref, k_ref, v_ref, kvb_ref, dt_ref, h_ref = refs[carried:]
    j = pl.program_id(1)

    @pl.when(j == 0)
    def _():
        h = _rms(x_ref[...], g_ref[...]).astype(BF16)
        h_ref[...] = h
        dt_ref[...] = _dot(h, wdt_ref[...])

    def owns(n):
        return jnp.logical_and(j >= _PROJ_FIRST[n], j < _PROJ_FIRST[n + 1])

    def proj():
        return _dot(h_ref[...], w_ref[...])

    @pl.when(owns(0))
    def _():
        zg_ref[...] = proj().astype(BF16)

    @pl.when(owns(1))
    def _():
        xbc_ref[...] = proj().astype(BF16)

    @pl.when(owns(2))
    def _():
        q_ref[...] = proj().astype(BF16)

    heads_per_tile = PROJ_TN // SB_HEAD_DIM
    tm = x_ref.shape[0]
    for n, rows_ref in ((3, k_ref), (4, v_ref)):
        for t in range(_PROJ_FIRST[n + 1] - _PROJ_FIRST[n]):
            @pl.when(j == _PROJ_FIRST[n] + t)
            def _(rows_ref=rows_ref, t=t):
                r = proj()
                kvb_ref[...] = r.astype(BF16)
                for hh in range(heads_per_tile):
                    head = t * heads_per_tile + hh
                    rows_ref[pl.ds(head, tm, stride=SB_HEADS), :] = r[:, hh * SB_HEAD_DIM:(hh + 1) * SB_HEAD_DIM]


def _in_proj(x, gain, w, w_dt, layer, kv_rows):
    m, k = x.shape
    tm = min(1024, m)
    tn = PROJ_TN
    bf16 = lambda n: jax.ShapeDtypeStruct((m, n), BF16)
    head_rows = jax.ShapeDtypeStruct((DEPTH * m * SB_HEADS, SB_HEAD_DIM), F32)

    def tile_of(n_first, n_last):
        first, count = _PROJ_FIRST[n_first], _PROJ_FIRST[n_last + 1] - _PROJ_FIRST[n_first]
        return pl.BlockSpec((tm, tn), lambda i, j: (i, jnp.clip(j - first, 0, count - 1)))

    first_tile = layer * (m // tm)
    rows_spec = pl.BlockSpec((tm * SB_HEADS, SB_HEAD_DIM), lambda i, j: (first_tile + i, 0))
    carried = tuple(kv_rows)
    in_specs = [pl.BlockSpec((tm, k), lambda i, j: (i, 0)),
                pl.BlockSpec((1, k), lambda i, j: (0, 0)),
                pl.BlockSpec((k, tn), lambda i, j: (0, j)),
                pl.BlockSpec((k, LANES), lambda i, j: (0, 0))]
    zg, xbc, q, k_rows, v_rows, kvb, dt_raw = pl.pallas_call(
        functools.partial(_in_proj_kernel, carried=len(carried)),
        out_shape=(bf16(2 * D_SSM), bf16(SSD_CONV_DIM), bf16(SB_WIDTH), head_rows, head_rows, bf16(2 * SB_WIDTH),
                   jax.ShapeDtypeStruct((m, LANES), F32)),
        grid=(m // tm, _PROJ_FIRST[-1]),
        in_specs=in_specs + [pl.BlockSpec(memory_space=pl.ANY)] * len(carried),
        out_specs=(tile_of(0, 0), tile_of(1, 1), tile_of(2, 2), rows_spec, rows_spec, tile_of(3, 4),
                   pl.BlockSpec((tm, LANES), lambda i, j: (i, 0))),
        scratch_shapes=[pltpu.VMEM((tm, k), BF16)],
        input_output_aliases={len(in_specs) + n: 3 + n for n in range(len(carried))},
        compiler_params=_cparams(("parallel", "arbitrary")),
        name="in_proj",
    )(x, gain, w, w_dt, *carried)
    return zg, xbc, q, kvb, dt_raw, (k_rows, v_rows)


def _ssd_kernel(xbc_ref, dt_ref, cst_ref, cw_ref, cb_ref, dtb_ref, alog_ref, dexp_ref, h0_ref, e_ref, tri_ref,
                shift_ref, y_ref, hout_ref, h_scr, carry_scr, xc_scr, dt_scr, *, cl, nchunk):
    l = pl.program_id(1)

    @pl.when(l == 0)
    def _():
        h_scr[...] = h0_ref[0]
        carry_scr[...] = cst_ref[0]

    xb = xbc_ref[0]
    xt = xb.astype(F32)
    tl = xt.shape[0]
    moved = _dot(shift_ref[...], xb)
    hist = carry_scr[...]
    row = lax.broadcasted_iota(jnp.int32, (SUBLANES, SSD_CONV_DIM), 0)
    conv = cb_ref[...] + xt * cw_ref[SSD_CONV - 1:SSD_CONV, :]
    head = jnp.zeros((SUBLANES, SSD_CONV_DIM), F32)
    for back in range(1, SSD_CONV):
        w_tap = cw_ref[SSD_CONV - 1 - back:SSD_CONV - back, :]
        conv = conv + moved[(back - 1) * tl:back * tl] * w_tap
        head = head + jnp.where(row < back, pltpu.roll(hist, back, 0), 0.0) * w_tap
    carry_scr[...] = xt[tl - SUBLANES:tl]
    xc_scr[...] = _silu(conv)
    xc_scr[0:SUBLANES, :] = _silu(conv[0:SUBLANES] + head)
    dtr = dt_ref[0] + dtb_ref[...]
    dt_scr[...] = jnp.maximum(dtr, 0.0) + jnp.log1p(jnp.exp(-jnp.abs(dtr)))

    a_head = -jnp.exp(alog_ref[...])
    tri = tri_ref[...]
    expand = e_ref[...]
    rowi = lax.broadcasted_iota(jnp.int32, (cl, cl), 0)
    coli = lax.broadcasted_iota(jnp.int32, (cl, cl), 1)
    causal = coli <= rowi
    lane = lax.broadcasted_iota(jnp.int32, (1, LANES), 1)
    half_masks = ((lane < SSD_HEAD_DIM).astype(F32), (lane >= SSD_HEAD_DIM).astype(F32))

    def expand_heads(v):
        return _dot(v.astype(BF16), expand)

    def chunk(r0):
        rows = pl.ds(r0, cl)
        xs = xc_scr[rows, 0:D_SSM]
        bm = xc_scr[rows, D_SSM:D_SSM + SSD_GROUPS * SSD_STATE]
        cm = xc_scr[rows, D_SSM + SSD_GROUPS * SSD_STATE:SSD_CONV_DIM]
        dtc = dt_scr[rows, :]
        a = dtc * a_head
        acum = sum(_dot(tri, p) for p in _split_bf16(a, 3))
        if cl < LANES:
            acum_sq = jnp.concatenate([acum, jnp.zeros((LANES - cl, LANES), F32)], axis=0)
        else:
            acum_sq = acum
        acum_t = acum_sq.T
        a_end = acum[cl - 1:cl, :]
        dt_e = expand_heads(dtc)
        dec_end_e = expand_heads(jnp.exp(a_end - acum))
        dec_in_e = expand_heads(jnp.exp(acum))
        chunk_decay = jnp.broadcast_to(jnp.exp(acum_t[:, cl - 1:cl]), (LANES, LANES))
        x_dt = xs * dt_e
        x_end = x_dt * dec_end_e
        groups = range(SSD_GROUPS)
        heads = range(SSD_HEADS)
        gss = [slice(g * SSD_STATE, (g + 1) * SSD_STATE) for g in groups]
        gws = [slice(g * GROUP_WIDTH, (g + 1) * GROUP_WIDTH) for g in groups]
        bgs = [bm[:, gs].astype(BF16) for gs in gss]
        cgs = [cm[:, gs].astype(BF16) for gs in gss]
        cbs = [_dot_nt(cg, bg) for cg, bg in zip(cgs, bgs)]
        y_offs = [_dot_nt(cg, h_scr[gw, :].astype(BF16)) for cg, gw in zip(cgs, gws)]
        sts = [_dot_tn(x_end[:, gw].astype(BF16), bg) for gw, bg in zip(gws, bgs)]
        m_es = []
        for e in heads:
            seg = acum[:, e:e + 1] - acum_t[e:e + 1, 0:cl]
            dec = jnp.exp(jnp.where(causal, seg, -jnp.inf))
            m_es.append((cbs[e // HEADS_PER_GROUP] * dec).astype(BF16))
        for pr in range(SSD_HEADS // 2):
            g = pr // (HEADS_PER_GROUP // 2)
            ps = slice(pr * LANES, (pr + 1) * LANES)
            pg = slice(pr * LANES - g * GROUP_WIDTH, (pr + 1) * LANES - g * GROUP_WIDTH)
            xp = x_dt[:, ps]
            yp = y_offs[g][:, pg] * dec_in_e[:, ps]
            for sub in range(2):
                yp = yp + _dot(m_es[2 * pr + sub], (xp * half_masks[sub]).astype(BF16))
            y_ref[0, rows, ps] = yp + dexp_ref[:, ps] * xs[:, ps]
        for e in heads:
            g = e // HEADS_PER_GROUP
            hs = slice(e * SSD_HEAD_DIM, (e + 1) * SSD_HEAD_DIM)
            ls = slice(e * SSD_HEAD_DIM - g * GROUP_WIDTH, (e + 1) * SSD_HEAD_DIM - g * GROUP_WIDTH)
            h_scr[hs, :] = h_scr[hs, :] * chunk_decay[e:e + 1, :] + sts[g][ls, :]

    if nchunk == 1:
        chunk(0)
    else:
        def body(c, carry):
            chunk(pl.multiple_of(c * cl, cl))
            return carry
        lax.fori_loop(0, nchunk, body, 0)

    @pl.when(l == pl.num_programs(1) - 1)
    def _():
        hout_ref[0] = h_scr[...]


def _ssd(xbc, dt_raw, conv_state8, conv_w, conv_b, dt_bias, a_log, d_exp, h0, expand, tri, *, tl, cl):
    bsz, length, _ = xbc.shape
    kernel = functools.partial(_ssd_kernel, cl=cl, nchunk=tl // cl)
    full2 = lambda b, l: (0, 0)
    src = jnp.arange(tl)[None, None, :]
    dst = jnp.arange(tl)[None, :, None]
    back = jnp.arange(1, SSD_CONV)[:, None, None]
    shift = (src == dst - back).astype(BF16).reshape((SSD_CONV - 1) * tl, tl)
    return pl.pallas_call(
        kernel,
        out_shape=(jax.ShapeDtypeStruct((bsz, length, D_SSM), F32),
                   jax.ShapeDtypeStruct((bsz, D_SSM, SSD_STATE), F32)),
        grid=(bsz, length // tl),
        in_specs=[pl.BlockSpec((1, tl, SSD_CONV_DIM), lambda b, l: (b, l, 0)),
                  pl.BlockSpec((1, tl, LANES), lambda b, l: (b, l, 0)),
                  pl.BlockSpec((1, SUBLANES, SSD_CONV_DIM), lambda b, l: (b, 0, 0)),
                  pl.BlockSpec((SSD_CONV, SSD_CONV_DIM), full2),
                  pl.BlockSpec((1, SSD_CONV_DIM), full2),
                  pl.BlockSpec((1, LANES), full2),
                  pl.BlockSpec((1, LANES), full2),
                  pl.BlockSpec((1, D_SSM), full2),
                  pl.BlockSpec((1, D_SSM, SSD_STATE), lambda b, l: (b, 0, 0)),
                  pl.BlockSpec((LANES, D_SSM), full2),
                  pl.BlockSpec((cl, cl), full2),
                  pl.BlockSpec(((SSD_CONV - 1) * tl, tl), full2)],
        out_specs=(pl.BlockSpec((1, tl, D_SSM), lambda b, l: (b, l, 0)),
                   pl.BlockSpec((1, D_SSM, SSD_STATE), lambda b, l: (b, 0, 0))),
        scratch_shapes=[pltpu.VMEM((D_SSM, SSD_STATE), F32),
                        pltpu.VMEM((SUBLANES, SSD_CONV_DIM), F32),
                        pltpu.VMEM((tl, SSD_CONV_DIM), F32),
                        pltpu.VMEM((tl, LANES), F32)],
        compiler_params=_cparams(("parallel", "arbitrary")),
        name="ssd",
    )(xbc, dt_raw, conv_state8, conv_w, conv_b, dt_bias, a_log, d_exp, h0, expand, tri, shift)


def _sb_blocks(qs, blocks, laters, accs, tri_ext):
    zs = [[_dot_nt(q, kb) * SB_SCALE for q, kb in zip(qs, kbs)] for kbs, _, _ in blocks]
    log_keeps, log_betas = [], []
    for z_row, (_, _, earlier) in zip(zs, blocks):
        keep_row, beta_row = [], []
        for z in z_row:
            l1p = jnp.log(1.0 + jnp.exp(-jnp.abs(z)))
            log_keep = -(jnp.maximum(z, 0.0) + l1p)
            if earlier is not None:
                log_keep = jnp.where(earlier, log_keep, 0.0)
            keep_row.append(log_keep)
            beta_row.append(jnp.minimum(z, 0.0) - l1p)
        log_keeps.append(keep_row)
        log_betas.append(beta_row)
    exts = [[_dot(lk.astype(BF16), tri_ext) for lk in keep_row] for keep_row in log_keeps]
    ws = []
    for beta_row, ext_row, (_, _, earlier) in zip(log_betas, exts, blocks):
        w_row = []
        for log_beta, ext, later in zip(beta_row, ext_row, laters):
            w = jnp.exp(log_beta + ext[:, :SB_BLOCK] + later)
            if earlier is not None:
                w = jnp.where(earlier, w, 0.0)
            w_row.append(w.astype(BF16))
        ws.append(w_row)
        laters = [later + ext[:, SB_BLOCK:] for later, ext in zip(laters, ext_row)]
    for w_row, (_, vbs, _) in zip(ws, blocks):
        accs = [acc + _dot(w, vb) for acc, w, vb in zip(accs, w_row, vbs)]
    return laters, accs


def _sb_cols(h):
    return slice(h * SB_HEAD_DIM, (h + 1) * SB_HEAD_DIM)


def _sb_alive(later_scr):
    return jnp.max(later_scr[...]) > -SB_DEAD_LOG


def _sb_heads(q_ref, blocks, later_scr, acc_scr, tri_ext, init):
    tq = q_ref.shape[1]
    for h0 in range(0, SB_HEADS, SB_HEAD_GROUP):
        heads = range(h0, h0 + SB_HEAD_GROUP)
        qs = [q_ref[0, :, _sb_cols(h)] for h in heads]
        if init:
            laters = [jnp.zeros((tq, SB_BLOCK), F32) for _ in heads]
            accs = [jnp.zeros((tq, SB_HEAD_DIM), F32) for _ in heads]
        else:
            laters = [later_scr[h] for h in heads]
            accs = [acc_scr[h] for h in heads]
        per_head = [([k_of(h) for h in heads], [v_of(h) for h in heads], earlier)
                    for k_of, v_of, earlier in blocks]
        laters, accs = _sb_blocks(qs, per_head, laters, accs, tri_ext)
        for h, later, acc in zip(heads, laters, accs):
            later_scr[h] = later
            acc_scr[h] = acc


def _sb_kv_of(read):
    return (lambda h: read(_sb_cols(h))), (lambda h: read(_sb_cols(SB_HEADS + h)))


def _sb_diag_mask(tq):
    key = lax.broadcasted_iota(jnp.int32, (tq, SB_BLOCK), 1)
    query = lax.broadcasted_iota(jnp.int32, (tq, SB_BLOCK), 0)
    return key < query


def _sb_prompt_kernel(q_ref, kv0_ref, kv1_ref, kv2_ref, kv_hbm, tri_ref, o_ref,
                      later_scr, acc_scr, kv_buf, sem):
    b = pl.program_id(0)
    qi = pl.program_id(1)
    tri_ext = tri_ref[...]
    state = (later_scr, acc_scr, tri_ext)

    k0, v0 = _sb_kv_of(lambda c: kv0_ref[0, :, c])
    diagonal = (k0, v0, _sb_diag_mask(q_ref.shape[1]))

    @pl.when(qi >= 2)
    def _():
        k1, v1 = _sb_kv_of(lambda c: kv1_ref[0, :, c])
        k2, v2 = _sb_kv_of(lambda c: kv2_ref[0, :, c])
        _sb_heads(q_ref, [diagonal, (k1, v1, None), (k2, v2, None)], *state, init=True)

    @pl.when(qi < 2)
    def _():
        _sb_heads(q_ref, [diagonal], *state, init=True)

    def more(carry):
        j, alive = carry
        return jnp.logical_and(j >= 0, alive)

    def body(carry):
        j, _ = carry
        rows = pl.ds(pl.multiple_of(j * SB_BLOCK, SB_BLOCK), SB_BLOCK)
        copy = pltpu.make_async_copy(kv_hbm.at[b, rows, :], kv_buf, sem)
        copy.start()
        copy.wait()
        kb, vb = _sb_kv_of(lambda c: kv_buf[:, c])
        _sb_heads(q_ref, [(kb, vb, None)], *state, init=False)
        return j - 1, _sb_alive(later_scr)

    lax.while_loop(more, body, (jnp.where(qi >= 2, qi - 3, qi - 1), _sb_alive(later_scr)))
    for h in range(SB_HEADS):
        o_ref[0, :, _sb_cols(h)] = acc_scr[h].astype(o_ref.dtype)


def _sb_prompt(q, kvb, tri_ext):
    bsz, length, _ = q.shape
    tq = SB_BLOCK
    blk = lambda back: pl.BlockSpec((1, SB_BLOCK, 2 * SB_WIDTH), lambda b, i: (b, jnp.maximum(i - back, 0), 0))
    return pl.pallas_call(
        _sb_prompt_kernel,
        out_shape=jax.ShapeDtypeStruct((bsz, length, SB_WIDTH), BF16),
        grid=(bsz, length // tq),
        in_specs=[pl.BlockSpec((1, tq, SB_WIDTH), lambda b, i: (b, i, 0)),
                  blk(0), blk(1), blk(2),
                  pl.BlockSpec(memory_space=pl.ANY),
                  pl.BlockSpec((SB_BLOCK, 2 * SB_BLOCK), lambda b, i: (0, 0))],
        out_specs=pl.BlockSpec((1, tq, SB_WIDTH), lambda b, i: (b, i, 0)),
        scratch_shapes=[pltpu.VMEM((SB_HEADS, tq, SB_BLOCK), F32),
                        pltpu.VMEM((SB_HEADS, tq, SB_HEAD_DIM), F32),
                        pltpu.VMEM((SB_BLOCK, 2 * SB_WIDTH), BF16),
                        pltpu.SemaphoreType.DMA(())],
        compiler_params=_cparams(("parallel", "arbitrary")),
        name="sb_prompt",
    )(q, kvb, kvb, kvb, kvb, tri_ext)


SB_BLOCK_ROWS = SB_BLOCK * SB_HEADS
SB_RECENT_BLOCKS = 2


def _sb_cache_heads(read):
    return lambda first_row: (lambda h: read(pl.ds(first_row + h, SB_BLOCK, stride=SB_HEADS)).astype(BF16))


def _sb_sample_kernel(q_ref, kvn_ref, kr_ref, vr_ref, k_hbm, v_hbm, tri_ref, o_ref,
                      later_scr, acc_scr, k_buf, v_buf, sem, *, npast, first):
    b = pl.program_id(0)
    tri_ext = tri_ref[...]
    state = (later_scr, acc_scr, tri_ext)
    kn, vn = _sb_kv_of(lambda c: kvn_ref[0, :, c])
    k_recent = _sb_cache_heads(lambda rows: kr_ref[0, rows, :])
    v_recent = _sb_cache_heads(lambda rows: vr_ref[0, rows, :])
    recent = [(k_recent(t * SB_BLOCK_ROWS), v_recent(t * SB_BLOCK_ROWS), None)
              for t in reversed(range(SB_RECENT_BLOCKS))]
    _sb_heads(q_ref, [(kn, vn, _sb_diag_mask(q_ref.shape[1]))] + recent, *state, init=True)

    def more(carry):
        j, alive = carry
        return jnp.logical_and(j >= 0, alive)

    def body(carry):
        j, _ = carry
        rows = pl.ds(pl.multiple_of(j * SB_BLOCK_ROWS, SB_BLOCK_ROWS), SB_BLOCK_ROWS)
        copies = [pltpu.make_async_copy(hbm.at[first + b, rows, :], buf, sem.at[n])
                  for n, (hbm, buf) in enumerate(((k_hbm, k_buf), (v_hbm, v_buf)))]
        for copy in copies:
            copy.start()
        for copy in copies:
            copy.wait()
        kb = _sb_cache_heads(lambda r: k_buf[r, :])(0)
        vb = _sb_cache_heads(lambda r: v_buf[r, :])(0)
        _sb_heads(q_ref, [(kb, vb, None)], *state, init=False)
        return j - 1, _sb_alive(later_scr)

    lax.while_loop(more, body, (npast - SB_RECENT_BLOCKS - 1, _sb_alive(later_scr)))
    for h in range(SB_HEADS):
        o_ref[0, :, _sb_cols(h)] = acc_scr[h].astype(o_ref.dtype)


def _sb_sample(q, kvb_new_pad, k_past, v_past, tri_ext, layer):
    bsz, lq, _ = q.shape
    rows = k_past.shape[1]
    npast = rows // SB_BLOCK_ROWS
    assert rows % SB_BLOCK_ROWS == 0 and npast >= SB_RECENT_BLOCKS and lq <= SB_BLOCK
    first = layer * bsz
    recent_rows = SB_RECENT_BLOCKS * SB_BLOCK_ROWS
    assert rows % recent_rows == 0
    recent = pl.BlockSpec((1, recent_rows, SB_HEAD_DIM), lambda b: (first + b, rows // recent_rows - 1, 0))
    return pl.pallas_call(
        functools.partial(_sb_sample_kernel, npast=npast, first=first),
        out_shape=jax.ShapeDtypeStruct((bsz, lq, SB_WIDTH), BF16),
        grid=(bsz,),
        in_specs=[pl.BlockSpec((1, lq, SB_WIDTH), lambda b: (b, 0, 0)),
                  pl.BlockSpec((1, SB_BLOCK, 2 * SB_WIDTH), lambda b: (b, 0, 0)),
                  recent, recent,
                  pl.BlockSpec(memory_space=pl.ANY),
                  pl.BlockSpec(memory_space=pl.ANY),
                  pl.BlockSpec((SB_BLOCK, 2 * SB_BLOCK), lambda b: (0, 0))],
        out_specs=pl.BlockSpec((1, lq, SB_WIDTH), lambda b: (b, 0, 0)),
        scratch_shapes=[pltpu.VMEM((SB_HEADS, lq, SB_BLOCK), F32),
                        pltpu.VMEM((SB_HEADS, lq, SB_HEAD_DIM), F32),
                        pltpu.VMEM((SB_BLOCK_ROWS, SB_HEAD_DIM), F32),
                        pltpu.VMEM((SB_BLOCK_ROWS, SB_HEAD_DIM), F32),
                        pltpu.SemaphoreType.DMA((2,))],
        compiler_params=_cparams(("parallel",)),
        name="sb_sample",
    )(q, kvb_new_pad, k_past, v_past, k_past, v_past, tri_ext)


def _merge_kernel(y_ref, z_ref, gate_ref, attn_ref, x_ref, nw_ref, w1_ref, w2_ref, w3_ref, gpost_ref, o_ref):
    z = z_ref[...].astype(F32)
    g = y_ref[...] * _silu(z)
    b1 = None
    for k in range(SSD_GROUPS):
        gw = slice(k * GROUP_WIDTH, (k + 1) * GROUP_WIDTH)
        gk = _rms(g[:, gw], nw_ref[:, gw])
        part = _dot(gk.astype(BF16), w1_ref[gw, :])
        b1 = part if b1 is None else b1 + part
    b2 = _dot(attn_ref[...], w2_ref[...])
    gl = gate_ref[...].astype(F32)
    m = _sigmoid(gl[:, :D_MODEL]) * b1 + _sigmoid(gl[:, D_MODEL:]) * b2
    mixed = _dot(m.astype(BF16), w3_ref[...])
    o_ref[...] = x_ref[...] + _rms(mixed, gpost_ref[...])


def _merge(y, zg, attn, x, ssd_norm, w_br_ssd, w_br_sb, w_out, norm_post):
    m = x.shape[0]
    tm = min(512, m)
    row = lambda i: (i, 0)
    full = lambda i: (0, 0)
    return pl.pallas_call(
        _merge_kernel,
        out_shape=jax.ShapeDtypeStruct((m, D_MODEL), F32),
        grid=(m // tm,),
        in_specs=[pl.BlockSpec((tm, D_SSM), row),
                  pl.BlockSpec((tm, D_SSM), row),
                  pl.BlockSpec((tm, 2 * D_MODEL), lambda i: (i, 1)),
                  pl.BlockSpec((tm, SB_WIDTH), row),
                  pl.BlockSpec((tm, D_MODEL), row),
                  pl.BlockSpec((1, D_SSM), full),
                  pl.BlockSpec((D_SSM, D_MODEL), full),
                  pl.BlockSpec((SB_WIDTH, D_MODEL), full),
                  pl.BlockSpec((D_MODEL, D_MODEL), full),
                  pl.BlockSpec((1, D_MODEL), full)],
        out_specs=pl.BlockSpec((tm, D_MODEL), row),
        compiler_params=_cparams(("parallel",)),
        name="merge",
    )(y, zg, zg, attn, x, ssd_norm, w_br_ssd, w_br_sb, w_out, norm_post)


def _ffn_kernel(x_ref, st_ref, gpre_ref, wg_ref, wu_ref, cw_ref, cb_ref, wd_ref, gpost_ref,
                p_ref, wple_ref, wpg_ref, gple_ref, o_ref, nst_ref, h_scr, acc_scr, carry_scr):
    l = pl.program_id(1)
    j = pl.program_id(2)
    nb, tl, _ = x_ref.shape

    @pl.when(j == 0)
    def _():
        h_scr[...] = _rms(x_ref[...].reshape(nb * tl, D_MODEL), gpre_ref[...]).astype(BF16)
        acc_scr[...] = jnp.zeros_like(acc_scr)

    @pl.when(l == 0)
    def _():
        carry_scr[j] = st_ref[...]

    h = h_scr[...]
    gate = _dot(h, wg_ref[...])
    up = _dot(h, wu_ref[...])
    conv = _causal_conv(gate, carry_scr[j], cw_ref[...], cb_ref[...])
    tail = gate.reshape(nb, tl, gate.shape[1])[:, tl - SUBLANES:, :]
    carry_scr[j] = tail

    @pl.when(l == pl.num_programs(1) - 1)
    def _():
        nst_ref[...] = tail

    half = 0.5 * conv
    gelu = half + half * jnp.tanh(conv * (GELU_C + (GELU_C * GELU_A) * (conv * conv)))
    acc_scr[...] += _dot((gelu * up).astype(BF16), wd_ref[...])

    @pl.when(j == pl.num_programs(2) - 1)
    def _():
        x1 = x_ref[...].reshape(nb * tl, D_MODEL) + _rms(acc_scr[...], gpost_ref[...])
        emb = _dot(p_ref[...].reshape(nb * tl, PLE_DIM).astype(BF16), wple_ref[...])
        ple = emb * _sigmoid(_dot(x1.astype(BF16), wpg_ref[...]))
        o_ref[...] = (x1 + _rms(ple, gple_ref[...])).reshape(nb, tl, D_MODEL)


def _ffn(x, state8, norm_pre, w_up, conv_w, conv_b, w_down, norm_post, p, w_ple, w_ple_gate, norm_ple,
         *, nb, tl, fc):
    bsz, length, _ = x.shape
    nf = D_FF // fc
    nl = length // tl
    full = lambda b, l, j: (0, 0)
    nst_map = lambda b, l, j: (b, 0, jnp.where(l == nl - 1, j, 0))
    return pl.pallas_call(
        _ffn_kernel,
        out_shape=(jax.ShapeDtypeStruct((bsz, length, D_MODEL), F32),
                   jax.ShapeDtypeStruct((bsz, SUBLANES, D_FF), F32)),
        grid=(bsz // nb, nl, nf),
        in_specs=[pl.BlockSpec((nb, tl, D_MODEL), lambda b, l, j: (b, l, 0)),
                  pl.BlockSpec((nb, SUBLANES, fc), lambda b, l, j: (b, 0, j)),
                  pl.BlockSpec((1, D_MODEL), full),
                  pl.BlockSpec((D_MODEL, fc), lambda b, l, j: (0, j)),
                  pl.BlockSpec((D_MODEL, fc), lambda b, l, j: (0, nf + j)),
                  pl.BlockSpec((FFN_CONV, fc), lambda b, l, j: (0, j)),
                  pl.BlockSpec((1, fc), lambda b, l, j: (0, j)),
                  pl.BlockSpec((fc, D_MODEL), lambda b, l, j: (j, 0)),
                  pl.BlockSpec((1, D_MODEL), full),
                  pl.BlockSpec((nb, tl, PLE_DIM), lambda b, l, j: (b, l, 0)),
                  pl.BlockSpec((PLE_DIM, D_MODEL), full),
                  pl.BlockSpec((D_MODEL, D_MODEL), full),
                  pl.BlockSpec((1, D_MODEL), full)],
        out_specs=(pl.BlockSpec((nb, tl, D_MODEL), lambda b, l, j: (b, l, 0)),
                   pl.BlockSpec((nb, SUBLANES, fc), nst_map)),
        scratch_shapes=[pltpu.VMEM((nb * tl, D_MODEL), BF16),
                        pltpu.VMEM((nb * tl, D_MODEL), F32),
                        pltpu.VMEM((nf, nb, SUBLANES, fc), F32)],
        compiler_params=_cparams(("parallel", "arbitrary", "arbitrary")),
        name="ffn",
    )(x, state8, norm_pre, w_up, w_up, conv_w, conv_b, w_down, norm_post, p, w_ple, w_ple_gate, norm_ple)


def _pad_rows_to8(state):
    return jnp.pad(state, ((0, 0), (SUBLANES - state.shape[1], 0), (0, 0)))


def _pad_lanes(v):
    return jnp.pad(v, (0, LANES - v.shape[0]))[None, :]


def _constants(cl):
    head = jnp.arange(LANES)[:, None]
    chan = jnp.arange(D_SSM)[None, :] // SSD_HEAD_DIM
    expand = (head == chan).astype(BF16)
    idx = jnp.arange(cl)
    tri_cum = (idx[None, :] <= idx[:, None]).astype(BF16)
    kidx = jnp.arange(SB_BLOCK)
    tri_in = (kidx[:, None] > kidx[None, :]).astype(BF16)
    tri_ext = jnp.concatenate([tri_in, jnp.ones((SB_BLOCK, SB_BLOCK), BF16)], axis=1)
    return expand, tri_cum, tri_ext


def _layer(x, p, conv_st, ssm_st, k_cache, v_cache, ffn_st, lw, layer, kv_rows):
    bsz, length, _ = x.shape
    m = bsz * length
    x2 = x.reshape(m, D_MODEL)
    cl = CHUNK if length >= CHUNK else length
    expand, tri_cum, tri_ext = _constants(cl)

    w_in = lw['w_in']
    o_z, o_xbc, o_dt, o_q, o_g = 0, D_SSM, D_SSM + SSD_CONV_DIM, D_SSM + SSD_CONV_DIM + SSD_HEADS, \
        D_SSM + SSD_CONV_DIM + SSD_HEADS + 3 * SB_WIDTH
    w_main = jnp.concatenate([w_in[:, o_z:o_xbc], w_in[:, o_g:], w_in[:, o_xbc:o_dt], w_in[:, o_q:o_g]],
                             axis=1).astype(BF16)
    w_dt = jnp.pad(w_in[:, o_dt:o_q], ((0, 0), (0, LANES - SSD_HEADS))).astype(BF16)
    g_pre = lw['norm_pre_mix'][None, :]
    zg, xbc, q, kvb, dt_raw, kv_rows = _in_proj(x2, g_pre, w_main, w_dt, layer, kv_rows)

    xbc3 = xbc.reshape(bsz, length, SSD_CONV_DIM)
    tl = min(256, length)
    y, h_new = _ssd(xbc3, dt_raw.reshape(bsz, length, LANES), _pad_rows_to8(conv_st),
                    lw['ssd_conv_w'], lw['ssd_conv_b'][None, :], _pad_lanes(lw['ssd_dt_bias']),
                    _pad_lanes(lw['ssd_a_log']), jnp.repeat(lw['ssd_d'], SSD_HEAD_DIM)[None, :],
                    ssm_st.reshape(bsz, D_SSM, SSD_STATE), expand, tri_cum, tl=tl, cl=cl)
    tail = 2 * SUBLANES
    x_tail = x[:, length - tail:, :].reshape(bsz * tail, D_MODEL)
    xbc_tail = _norm_matmul(x_tail, g_pre, w_in[:, o_xbc:o_dt].astype(BF16))
    new_conv = xbc_tail.reshape(bsz, tail, SSD_CONV_DIM)[:, tail - (SSD_CONV - 1):]
    new_ssm = h_new.reshape(bsz, SSD_HEADS, SSD_HEAD_DIM, SSD_STATE)

    q3 = q.reshape(bsz, length, SB_WIDTH)
    kvb3 = kvb.reshape(bsz, length, 2 * SB_WIDTH)
    if k_cache is None:
        attn = _sb_prompt(q3, kvb3, tri_ext)
    else:
        depth, _, past = k_cache.shape[:3]
        kvb_pad = jnp.pad(kvb3, ((0, 0), (0, SB_BLOCK - length), (0, 0)))
        attn = _sb_sample(q3, kvb_pad, k_cache.reshape(depth * bsz, past * SB_HEADS, SB_HEAD_DIM),
                          v_cache.reshape(depth * bsz, past * SB_HEADS, SB_HEAD_DIM), tri_ext, layer)

    x2 = _merge(y.reshape(m, D_SSM), zg, attn.reshape(m, SB_WIDTH), x2, lw['ssd_norm'][None, :],
                lw['w_br_ssd'].astype(BF16), lw['w_br_sb'].astype(BF16), lw['w_out'].astype(BF16),
                lw['norm_post_mix'][None, :])

    x3, ffn8 = _ffn(x2.reshape(bsz, length, D_MODEL), _pad_rows_to8(ffn_st), lw['norm_pre_ffn'][None, :],
                    lw['w_up'].astype(BF16), lw['ffn_conv_w'], lw['ffn_conv_b'][None, :],
                    lw['w_down'].astype(BF16), lw['norm_post_ffn'][None, :],
                    p, lw['w_ple'].astype(BF16), lw['w_ple_gate'].astype(BF16), lw['norm_ple'][None, :],
                    nb=max(1, min(bsz, FFN_ROWS // length)), tl=min(FFN_ROWS, length), fc=512)
    new_ffn = ffn8[:, SUBLANES - (FFN_CONV - 1):]
    return x3, (new_conv, new_ssm, new_ffn), kv_rows


_LAYER_WEIGHTS = ('norm_pre_mix', 'w_in', 'ssd_conv_w', 'ssd_conv_b', 'ssd_dt_bias', 'ssd_a_log', 'ssd_d',
                  'ssd_norm', 'w_br_ssd', 'w_br_sb', 'w_out', 'norm_post_mix', 'norm_pre_ffn', 'w_up',
                  'ffn_conv_w', 'ffn_conv_b', 'w_down', 'norm_post_ffn', 'w_ple', 'w_ple_gate', 'norm_ple')


def _run(x, p, conv0, ssm0, k0, v0, ffn0, weights):
    bsz, length, _ = x.shape
    per_layer = []
    kv_rows = tuple(jnp.zeros((DEPTH * bsz * length * SB_HEADS, SB_HEAD_DIM), F32) for _ in range(2))
    for i in range(DEPTH):
        lw = {name: weights[name][i] for name in _LAYER_WEIGHTS}
        x, st, kv_rows = _layer(x, p[i], conv0[i], ssm0[i], k0, v0, ffn0[i], lw, i, kv_rows)
        per_layer.append(st)
    new_conv, new_ssm, new_ffn = (jnp.stack([st[j] for st in per_layer]) for j in range(3))
    new_k, new_v = (rows.reshape(DEPTH, bsz, length, SB_HEADS, SB_HEAD_DIM) for rows in kv_rows)
    return x, [new_conv, new_ssm, new_k, new_v, new_ffn]


def kernel(x_prompt, x_sample, state_ssd_conv, state_ssd, cache_sb_k, cache_sb_v, state_ffn_conv, p_prompt, p_sample, norm_pre_mix, w_in, ssd_conv_w, ssd_conv_b, ssd_dt_bias, ssd_a_log, ssd_d, ssd_norm, w_br_ssd, w_br_sb, w_out, norm_post_mix, norm_pre_ffn, w_up, ffn_conv_w, ffn_conv_b, w_down, norm_post_ffn, w_ple, w_ple_gate, norm_ple):
    weights = dict(norm_pre_mix=norm_pre_mix, w_in=w_in, ssd_conv_w=ssd_conv_w, ssd_conv_b=ssd_conv_b,
                   ssd_dt_bias=ssd_dt_bias, ssd_a_log=ssd_a_log, ssd_d=ssd_d, ssd_norm=ssd_norm,
                   w_br_ssd=w_br_ssd, w_br_sb=w_br_sb, w_out=w_out, norm_post_mix=norm_post_mix,
                   norm_pre_ffn=norm_pre_ffn, w_up=w_up, ffn_conv_w=ffn_conv_w, ffn_conv_b=ffn_conv_b,
                   w_down=w_down, norm_post_ffn=norm_post_ffn, w_ple=w_ple, w_ple_gate=w_ple_gate,
                   norm_ple=norm_ple)
    bp = x_prompt.shape[0]
    zero_conv = jnp.zeros((DEPTH, bp, SSD_CONV - 1, SSD_CONV_DIM), F32)
    zero_ssm = jnp.zeros((DEPTH, bp, SSD_HEADS, SSD_HEAD_DIM, SSD_STATE), F32)
    zero_ffn = jnp.zeros((DEPTH, bp, FFN_CONV - 1, D_FF), F32)
    y_prompt, ps = _run(x_prompt, p_prompt, zero_conv, zero_ssm, None, None, zero_ffn, weights)
    y_sample, ss = _run(x_sample, p_sample, state_ssd_conv, state_ssd, cache_sb_k, cache_sb_v,
                        state_ffn_conv, weights)
    return (y_prompt, y_sample, ps[0], ps[1], ps[2], ps[3], ps[4], ss[0], ss[1], ss[2], ss[3], ss[4])
```

```python
import functools

import jax
import jax.numpy as jnp
from jax import lax
from jax.experimental import pallas as pl
from jax.experimental.pallas import tpu as pltpu

F32 = jnp.float32
BF16 = jnp.bfloat16

D_MODEL = 1024
DEPTH = 2
CHUNK = 64
D_SSM = 2048
SSD_HEAD_DIM = 64
SSD_HEADS = 32
SSD_GROUPS = 4
HEADS_PER_GROUP = 8
SSD_STATE = 128
SSD_CONV = 4
SSD_CONV_DIM = D_SSM + 2 * SSD_GROUPS * SSD_STATE
GROUP_WIDTH = D_SSM // SSD_GROUPS
SB_HEAD_DIM = 128
SB_HEADS = 8
SB_WIDTH = 1024
SB_BLOCK = 128
D_FF = 4096
FFN_CONV = 3
PLE_DIM = 256
EPS = 1e-6
GELU_C = 0.7978845608028654
GELU_A = 0.044715
SB_SCALE = SB_HEAD_DIM ** -0.5
SB_HEAD_GROUP = 8
SB_DEAD_LOG = 120.0

LANES = 128
SUBLANES = 8
VMEM_LIMIT = 56 * 1024 * 1024
FFN_ROWS = 1024


def _cparams(sem):
    return pltpu.CompilerParams(dimension_semantics=sem, vmem_limit_bytes=VMEM_LIMIT)


def _rms(x, gain):
    return x * lax.rsqrt(jnp.mean(x * x, axis=-1, keepdims=True) + EPS) * gain


def _sigmoid(x):
    return 0.5 + 0.5 * jnp.tanh(0.5 * x)


def _silu(x):
    h = 0.5 * x
    return h + h * jnp.tanh(h)


def _dot(a, b):
    return jnp.dot(a, b, preferred_element_type=F32)


def _dot_nt(a, b):
    return lax.dot_general(a, b, (((1,), (1,)), ((), ())), preferred_element_type=F32)


def _dot_tn(a, b):
    return lax.dot_general(a, b, (((0,), (0,)), ((), ())), preferred_element_type=F32)


def _split_bf16(v, n):
    parts = []
    r = v
    for _ in range(n):
        p = r.astype(BF16)
        parts.append(p)
        r = r - p.astype(F32)
    return parts


def _shifted_rows(x, hist, k):
    nseq = hist.shape[0]
    tl = x.shape[0] // nseq
    rolled = pltpu.roll(x, k, 0)
    row = lax.broadcasted_iota(jnp.int32, (SUBLANES, x.shape[1]), 0)
    pieces = []
    for b in range(nseq):
        start = b * tl
        pieces.append(jnp.where(row < k, pltpu.roll(hist[b], k, 0), rolled[start:start + SUBLANES]))
        if tl > SUBLANES:
            pieces.append(rolled[start + SUBLANES:start + tl])
    return pieces[0] if len(pieces) == 1 else jnp.concatenate(pieces, axis=0)


def _causal_conv(x, hist, w, b):
    width = w.shape[0]
    acc = b
    for tap in range(width):
        k = width - 1 - tap
        sh = x if k == 0 else _shifted_rows(x, hist, k)
        acc = acc + sh * w[tap:tap + 1, :]
    return acc


def _norm_matmul_kernel(x_ref, g_ref, w_ref, o_ref, h_ref):
    @pl.when(pl.program_id(1) == 0)
    def _():
        h_ref[...] = _rms(x_ref[...], g_ref[...]).astype(BF16)

    o_ref[...] = _dot(h_ref[...], w_ref[...])


def _norm_matmul(x, gain, w):
    m, k = x.shape
    n = w.shape[1]
    tm = min(1024, m)
    tn = min(1024, n)
    return pl.pallas_call(
        _norm_matmul_kernel,
        out_shape=jax.ShapeDtypeStruct((m, n), F32),
        grid=(m // tm, n // tn),
        in_specs=[pl.BlockSpec((tm, k), lambda i, j: (i, 0)),
                  pl.BlockSpec((1, k), lambda i, j: (0, 0)),
                  pl.BlockSpec((k, tn), lambda i, j: (0, j))],
        out_specs=pl.BlockSpec((tm, tn), lambda i, j: (i, j)),
        scratch_shapes=[pltpu.VMEM((tm, k), BF16)],
        compiler_params=_cparams(("parallel", "arbitrary")),
        name="norm_matmul",
    )(x, gain, w)


PROJ_TN = 1024
_PROJ_WIDTHS = (2 * D_SSM, SSD_CONV_DIM, SB_WIDTH, SB_WIDTH, SB_WIDTH)
_PROJ_FIRST = tuple(sum(_PROJ_WIDTHS[:n]) // PROJ_TN for n in range(len(_PROJ_WIDTHS) + 1))


def _in_proj_kernel(x_ref, g_ref, w_ref, wdt_ref, *refs, carried):
    zg_ref, xbc_ref, q_ref, k_ref, v_ref, kvb_ref, dt_ref, h_ref = refs[carried:]
    j = pl.program_id(1)

    @pl.when(j == 0)
    def _():
        h = _rms(x_ref[...], g_ref[...]).astype(BF16)
        h_ref[...] = h
        dt_ref[...] = _dot(h, wdt_ref[...])

    def owns(n):
        return jnp.logical_and(j >= _PROJ_FIRST[n], j < _PROJ_FIRST[n + 1])

    def proj():
        return _dot(h_ref[...], w_ref[...])

    @pl.when(owns(0))
    def _():
        zg_ref[...] = proj().astype(BF16)

    @pl.when(owns(1))
    def _():
        xbc_ref[...] = proj().astype(BF16)

    @pl.when(owns(2))
    def _():
        q_ref[...] = proj().astype(BF16)

    heads_per_tile = PROJ_TN // SB_HEAD_DIM
    tm = x_ref.shape[0]
    for n, rows_ref in ((3, k_ref), (4, v_ref)):
        for t in range(_PROJ_FIRST[n + 1] - _PROJ_FIRST[n]):
            @pl.when(j == _PROJ_FIRST[n] + t)
            def _(rows_ref=rows_ref, t=t):
                r = proj()
                kvb_ref[...] = r.astype(BF16)
                for hh in range(heads_per_tile):
                    head = t * heads_per_tile + hh
                    rows_ref[pl.ds(head, tm, stride=SB_HEADS), :] = r[:, hh * SB_HEAD_DIM:(hh + 1) * SB_HEAD_DIM]


def _in_proj(x, gain, w, w_dt, layer, kv_rows):
    m, k = x.shape
    tm = min(1024, m)
    tn = PROJ_TN
    bf16 = lambda n: jax.ShapeDtypeStruct((m, n), BF16)
    head_rows = jax.ShapeDtypeStruct((DEPTH * m * SB_HEADS, SB_HEAD_DIM), F32)

    def tile_of(n_first, n_last):
        first, count = _PROJ_FIRST[n_first], _PROJ_FIRST[n_last + 1] - _PROJ_FIRST[n_first]
        return pl.BlockSpec((tm, tn), lambda i, j: (i, jnp.clip(j - first, 0, count - 1)))

    first_tile = layer * (m // tm)
    rows_spec = pl.BlockSpec((tm * SB_HEADS, SB_HEAD_DIM), lambda i, j: (first_tile + i, 0))
    carried = tuple(kv_rows)
    in_specs = [pl.BlockSpec((tm, k), lambda i, j: (i, 0)),
                pl.BlockSpec((1, k), lambda i, j: (0, 0)),
                pl.BlockSpec((k, tn), lambda i, j: (0, j)),
                pl.BlockSpec((k, LANES), lambda i, j: (0, 0))]
    zg, xbc, q, k_rows, v_rows, kvb, dt_raw = pl.pallas_call(
        functools.partial(_in_proj_kernel, carried=len(carried)),
        out_shape=(bf16(2 * D_SSM), bf16(SSD_CONV_DIM), bf16(SB_WIDTH), head_rows, head_rows, bf16(2 * SB_WIDTH),
                   jax.ShapeDtypeStruct((m, LANES), F32)),
        grid=(m // tm, _PROJ_FIRST[-1]),
        in_specs=in_specs + [pl.BlockSpec(memory_space=pl.ANY)] * len(carried),
        out_specs=(tile_of(0, 0), tile_of(1, 1), tile_of(2, 2), rows_spec, rows_spec, tile_of(3, 4),
                   pl.BlockSpec((tm, LANES), lambda i, j: (i, 0))),
        scratch_shapes=[pltpu.VMEM((tm, k), BF16)],
        input_output_aliases={len(in_specs) + n: 3 + n for n in range(len(carried))},
        compiler_params=_cparams(("parallel", "arbitrary")),
        name="in_proj",
    )(x, gain, w, w_dt, *carried)
    return zg, xbc, q, kvb, dt_raw, (k_rows, v_rows)


def _ssd_kernel(xbc_ref, dt_ref, cst_ref, cw_ref, cb_ref, dtb_ref, alog_ref, dexp_ref, h0_ref, e_ref, tri_ref,
                shift_ref, hall_ref, y_ref, hout_ref, h_scr, carry_scr, xc_scr, dt_scr, *, cl, nchunk):
    l = pl.program_id(1)

    @pl.when(l == 0)
    def _():
        h_scr[...] = h0_ref[0]
        carry_scr[...] = cst_ref[0]

    xb = xbc_ref[0]
    xt = xb.astype(F32)
    tl = xt.shape[0]
    moved = _dot(shift_ref[...], xb)
    hist = carry_scr[...]
    row = lax.broadcasted_iota(jnp.int32, (SUBLANES, SSD_CONV_DIM), 0)
    conv = cb_ref[...] + xt * cw_ref[SSD_CONV - 1:SSD_CONV, :]
    head = jnp.zeros((SUBLANES, SSD_CONV_DIM), F32)
    for back in range(1, SSD_CONV):
        w_tap = cw_ref[SSD_CONV - 1 - back:SSD_CONV - back, :]
        conv = conv + moved[(back - 1) * tl:back * tl] * w_tap
        head = head + jnp.where(row < back, pltpu.roll(hist, back, 0), 0.0) * w_tap
    carry_scr[...] = xt[tl - SUBLANES:tl]
    xc_scr[...] = _silu(conv)
    xc_scr[0:SUBLANES, :] = _silu(conv[0:SUBLANES] + head)
    dtr = dt_ref[0] + dtb_ref[...]
    dt_scr[...] = jnp.maximum(dtr, 0.0) + jnp.log1p(jnp.exp(-jnp.abs(dtr)))

    a_head = -jnp.exp(alog_ref[...])
    tri = tri_ref[...]
    expand = e_ref[...]
    rowi = lax.broadcasted_iota(jnp.int32, (cl, cl), 0)
    coli = lax.broadcasted_iota(jnp.int32, (cl, cl), 1)
    causal = coli <= rowi
    lane = lax.broadcasted_iota(jnp.int32, (1, LANES), 1)
    half_masks = ((lane < SSD_HEAD_DIM).astype(F32), (lane >= SSD_HEAD_DIM).astype(F32))

    def expand_heads(v):
        return _dot(v.astype(BF16), expand)

    def chunk(r0):
        rows = pl.ds(r0, cl)
        xs = xc_scr[rows, 0:D_SSM]
        bm = xc_scr[rows, D_SSM:D_SSM + SSD_GROUPS * SSD_STATE]
        cm = xc_scr[rows, D_SSM + SSD_GROUPS * SSD_STATE:SSD_CONV_DIM]
        dtc = dt_scr[rows, :]
        a = dtc * a_head
        acum = sum(_dot(tri, p) for p in _split_bf16(a, 3))
        if cl < LANES:
            acum_sq = jnp.concatenate([acum, jnp.zeros((LANES - cl, LANES), F32)], axis=0)
        else:
            acum_sq = acum
        acum_t = acum_sq.T
        a_end = acum[cl - 1:cl, :]
        dt_e = expand_heads(dtc)
        dec_end_e = expand_heads(jnp.exp(a_end - acum))
        dec_in_e = expand_heads(jnp.exp(acum))
        chunk_decay = jnp.broadcast_to(jnp.exp(acum_t[:, cl - 1:cl]), (LANES, LANES))
        x_dt = xs * dt_e
        x_end = x_dt * dec_end_e
        groups = range(SSD_GROUPS)
        heads = range(SSD_HEADS)
        gss = [slice(g * SSD_STATE, (g + 1) * SSD_STATE) for g in groups]
        gws = [slice(g * GROUP_WIDTH, (g + 1) * GROUP_WIDTH) for g in groups]
        bgs = [bm[:, gs].astype(BF16) for gs in gss]
        cgs = [cm[:, gs].astype(BF16) for gs in gss]
        cbs = [_dot_nt(cg, bg) for cg, bg in zip(cgs, bgs)]
        y_offs = [_dot_nt(cg, h_scr[gw, :].astype(BF16)) for cg, gw in zip(cgs, gws)]
        sts = [_dot_tn(x_end[:, gw].astype(BF16), bg) for gw, bg in zip(gws, bgs)]
        m_es = []
        for e in heads:
            seg = acum[:, e:e + 1] - acum_t[e:e + 1, 0:cl]
            dec = jnp.exp(jnp.where(causal, seg, -jnp.inf))
            m_es.append((cbs[e // HEADS_PER_GROUP] * dec).astype(BF16))
        for pr in range(SSD_HEADS // 2):
            g = pr // (HEADS_PER_GROUP // 2)
            ps = slice(pr * LANES, (pr + 1) * LANES)
            pg = slice(pr * LANES - g * GROUP_WIDTH, (pr + 1) * LANES - g * GROUP_WIDTH)
            xp = x_dt[:, ps]
            yp = y_offs[g][:, pg] * dec_in_e[:, ps]
            for sub in range(2):
                yp = yp + _dot(m_es[2 * pr + sub], (xp * half_masks[sub]).astype(BF16))
            y_ref[0, rows, ps] = yp + dexp_ref[:, ps] * xs[:, ps]
        for e in heads:
            g = e // HEADS_PER_GROUP
            hs = slice(e * SSD_HEAD_DIM, (e + 1) * SSD_HEAD_DIM)
            ls = slice(e * SSD_HEAD_DIM - g * GROUP_WIDTH, (e + 1) * SSD_HEAD_DIM - g * GROUP_WIDTH)
            h_scr[hs, :] = h_scr[hs, :] * chunk_decay[e:e + 1, :] + sts[g][ls, :]

    if nchunk == 1:
        chunk(0)
    else:
        def body(c, carry):
            chunk(pl.multiple_of(c * cl, cl))
            return carry
        lax.fori_loop(0, nchunk, body, 0)

    @pl.when(l == pl.num_programs(1) - 1)
    def _():
        hout_ref[0] = h_scr[...]


def _ssd(xbc, dt_raw, conv_state8, conv_w, conv_b, dt_bias, a_log, d_exp, h0, expand, tri, h_all, layer, *, tl, cl):
    bsz, length, _ = xbc.shape
    first = layer * bsz
    kernel = functools.partial(_ssd_kernel, cl=cl, nchunk=tl // cl)
    full2 = lambda b, l: (0, 0)
    src = jnp.arange(tl)[None, None, :]
    dst = jnp.arange(tl)[None, :, None]
    back = jnp.arange(1, SSD_CONV)[:, None, None]
    shift = (src == dst - back).astype(BF16).reshape((SSD_CONV - 1) * tl, tl)
    return pl.pallas_call(
        kernel,
        out_shape=(jax.ShapeDtypeStruct((bsz, length, D_SSM), F32),
                   jax.ShapeDtypeStruct(h_all.shape, F32)),
        grid=(bsz, length // tl),
        in_specs=[pl.BlockSpec((1, tl, SSD_CONV_DIM), lambda b, l: (b, l, 0)),
                  pl.BlockSpec((1, tl, LANES), lambda b, l: (b, l, 0)),
                  pl.BlockSpec((1, SUBLANES, SSD_CONV_DIM), lambda b, l: (b, 0, 0)),
                  pl.BlockSpec((SSD_CONV, SSD_CONV_DIM), full2),
                  pl.BlockSpec((1, SSD_CONV_DIM), full2),
                  pl.BlockSpec((1, LANES), full2),
                  pl.BlockSpec((1, LANES), full2),
                  pl.BlockSpec((1, D_SSM), full2),
                  pl.BlockSpec((1, D_SSM, SSD_STATE), lambda b, l: (b, 0, 0)),
                  pl.BlockSpec((LANES, D_SSM), full2),
                  pl.BlockSpec((cl, cl), full2),
                  pl.BlockSpec(((SSD_CONV - 1) * tl, tl), full2),
                  pl.BlockSpec(memory_space=pl.ANY)],
        out_specs=(pl.BlockSpec((1, tl, D_SSM), lambda b, l: (b, l, 0)),
                   pl.BlockSpec((1, D_SSM, SSD_STATE), lambda b, l: (first + b, 0, 0))),
        input_output_aliases={12: 1},
        scratch_shapes=[pltpu.VMEM((D_SSM, SSD_STATE), F32),
                        pltpu.VMEM((SUBLANES, SSD_CONV_DIM), F32),
                        pltpu.VMEM((tl, SSD_CONV_DIM), F32),
                        pltpu.VMEM((tl, LANES), F32)],
        compiler_params=_cparams(("parallel", "arbitrary")),
        name="ssd",
    )(xbc, dt_raw, conv_state8, conv_w, conv_b, dt_bias, a_log, d_exp, h0, expand, tri, shift, h_all)


def _sb_blocks(qs, blocks, laters, accs, tri_ext):
    zs = [[_dot_nt(q, kb) * SB_SCALE for q, kb in zip(qs, kbs)] for kbs, _, _ in blocks]
    log_keeps, log_betas = [], []
    for z_row, (_, _, earlier) in zip(zs, blocks):
        keep_row, beta_row = [], []
        for z in z_row:
            l1p = jnp.log(1.0 + jnp.exp(-jnp.abs(z)))
            log_keep = -(jnp.maximum(z, 0.0) + l1p)
            if earlier is not None:
                log_keep = jnp.where(earlier, log_keep, 0.0)
            keep_row.append(log_keep)
            beta_row.append(jnp.minimum(z, 0.0) - l1p)
        log_keeps.append(keep_row)
        log_betas.append(beta_row)
    exts = [[_dot(lk.astype(BF16), tri_ext) for lk in keep_row] for keep_row in log_keeps]
    ws = []
    for beta_row, ext_row, (_, _, earlier) in zip(log_betas, exts, blocks):
        w_row = []
        for log_beta, ext, later in zip(beta_row, ext_row, laters):
            w = jnp.exp(log_beta + ext[:, :SB_BLOCK] + later)
            if earlier is not None:
                w = jnp.where(earlier, w, 0.0)
            w_row.append(w.astype(BF16))
        ws.append(w_row)
        laters = [later + ext[:, SB_BLOCK:] for later, ext in zip(laters, ext_row)]
    for w_row, (_, vbs, _) in zip(ws, blocks):
        accs = [acc + _dot(w, vb) for acc, w, vb in zip(accs, w_row, vbs)]
    return laters, accs


def _sb_cols(h):
    return slice(h * SB_HEAD_DIM, (h + 1) * SB_HEAD_DIM)


def _sb_alive(later_scr):
    return jnp.max(later_scr[...]) > -SB_DEAD_LOG


def _sb_heads(q_ref, blocks, later_scr, acc_scr, tri_ext, init):
    tq = q_ref.shape[1]
    for h0 in range(0, SB_HEADS, SB_HEAD_GROUP):
        heads = range(h0, h0 + SB_HEAD_GROUP)
        qs = [q_ref[0, :, _sb_cols(h)] for h in heads]
        if init:
            laters = [jnp.zeros((tq, SB_BLOCK), F32) for _ in heads]
            accs = [jnp.zeros((tq, SB_HEAD_DIM), F32) for _ in heads]
        else:
            laters = [later_scr[h] for h in heads]
            accs = [acc_scr[h] for h in heads]
        per_head = [([k_of(h) for h in heads], [v_of(h) for h in heads], earlier)
                    for k_of, v_of, earlier in blocks]
        laters, accs = _sb_blocks(qs, per_head, laters, accs, tri_ext)
        for h, later, acc in zip(heads, laters, accs):
            later_scr[h] = later
            acc_scr[h] = acc


def _sb_kv_of(read):
    return (lambda h: read(_sb_cols(h))), (lambda h: read(_sb_cols(SB_HEADS + h)))


def _sb_diag_mask(tq):
    key = lax.broadcasted_iota(jnp.int32, (tq, SB_BLOCK), 1)
    query = lax.broadcasted_iota(jnp.int32, (tq, SB_BLOCK), 0)
    return key < query


def _sb_prompt_kernel(q_ref, kv0_ref, kv1_ref, kv2_ref, kv_hbm, tri_ref, o_ref,
                      later_scr, acc_scr, kv_buf, sem):
    b = pl.program_id(0)
    qi = pl.program_id(1)
    tri_ext = tri_ref[...]
    state = (later_scr, acc_scr, tri_ext)

    k0, v0 = _sb_kv_of(lambda c: kv0_ref[0, :, c])
    diagonal = (k0, v0, _sb_diag_mask(q_ref.shape[1]))

    @pl.when(qi >= 2)
    def _():
        k1, v1 = _sb_kv_of(lambda c: kv1_ref[0, :, c])
        k2, v2 = _sb_kv_of(lambda c: kv2_ref[0, :, c])
        _sb_heads(q_ref, [diagonal, (k1, v1, None), (k2, v2, None)], *state, init=True)

    @pl.when(qi < 2)
    def _():
        _sb_heads(q_ref, [diagonal], *state, init=True)

    def more(carry):
        j, alive = carry
        return jnp.logical_and(j >= 0, alive)

    def body(carry):
        j, _ = carry
        rows = pl.ds(pl.multiple_of(j * SB_BLOCK, SB_BLOCK), SB_BLOCK)
        copy = pltpu.make_async_copy(kv_hbm.at[b, rows, :], kv_buf, sem)
        copy.start()
        copy.wait()
        kb, vb = _sb_kv_of(lambda c: kv_buf[:, c])
        _sb_heads(q_ref, [(kb, vb, None)], *state, init=False)
        return j - 1, _sb_alive(later_scr)

    lax.while_loop(more, body, (jnp.where(qi >= 2, qi - 3, qi - 1), _sb_alive(later_scr)))
    for h in range(SB_HEADS):
        o_ref[0, :, _sb_cols(h)] = acc_scr[h].astype(o_ref.dtype)


def _sb_prompt(q, kvb, tri_ext):
    bsz, length, _ = q.shape
    tq = SB_BLOCK
    blk = lambda back: pl.BlockSpec((1, SB_BLOCK, 2 * SB_WIDTH), lambda b, i: (b, jnp.maximum(i - back, 0), 0))
    return pl.pallas_call(
        _sb_prompt_kernel,
        out_shape=jax.ShapeDtypeStruct((bsz, length, SB_WIDTH), BF16),
        grid=(bsz, length // tq),
        in_specs=[pl.BlockSpec((1, tq, SB_WIDTH), lambda b, i: (b, i, 0)),
                  blk(0), blk(1), blk(2),
                  pl.BlockSpec(memory_space=pl.ANY),
                  pl.BlockSpec((SB_BLOCK, 2 * SB_BLOCK), lambda b, i: (0, 0))],
        out_specs=pl.BlockSpec((1, tq, SB_WIDTH), lambda b, i: (b, i, 0)),
        scratch_shapes=[pltpu.VMEM((SB_HEADS, tq, SB_BLOCK), F32),
                        pltpu.VMEM((SB_HEADS, tq, SB_HEAD_DIM), F32),
                        pltpu.VMEM((SB_BLOCK, 2 * SB_WIDTH), BF16),
                        pltpu.SemaphoreType.DMA(())],
        compiler_params=_cparams(("parallel", "arbitrary")),
        name="sb_prompt",
    )(q, kvb, kvb, kvb, kvb, tri_ext)


SB_BLOCK_ROWS = SB_BLOCK * SB_HEADS
SB_RECENT_BLOCKS = 2


def _sb_cache_heads(read):
    return lambda first_row: (lambda h: read(pl.ds(first_row + h, SB_BLOCK, stride=SB_HEADS)).astype(BF16))


def _sb_sample_kernel(q_ref, kvn_ref, kr_ref, vr_ref, k_hbm, v_hbm, tri_ref, o_ref,
                      later_scr, acc_scr, k_buf, v_buf, sem, *, npast, first):
    b = pl.program_id(0)
    tri_ext = tri_ref[...]
    state = (later_scr, acc_scr, tri_ext)
    kn, vn = _sb_kv_of(lambda c: kvn_ref[0, :, c])
    k_recent = _sb_cache_heads(lambda rows: kr_ref[0, rows, :])
    v_recent = _sb_cache_heads(lambda rows: vr_ref[0, rows, :])
    recent = [(k_recent(t * SB_BLOCK_ROWS), v_recent(t * SB_BLOCK_ROWS), None)
              for t in reversed(range(SB_RECENT_BLOCKS))]
    _sb_heads(q_ref, [(kn, vn, _sb_diag_mask(q_ref.shape[1]))] + recent, *state, init=True)

    def more(carry):
        j, alive = carry
        return jnp.logical_and(j >= 0, alive)

    def body(carry):
        j, _ = carry
        rows = pl.ds(pl.multiple_of(j * SB_BLOCK_ROWS, SB_BLOCK_ROWS), SB_BLOCK_ROWS)
        copies = [pltpu.make_async_copy(hbm.at[first + b, rows, :], buf, sem.at[n])
                  for n, (hbm, buf) in enumerate(((k_hbm, k_buf), (v_hbm, v_buf)))]
        for copy in copies:
            copy.start()
        for copy in copies:
            copy.wait()
        kb = _sb_cache_heads(lambda r: k_buf[r, :])(0)
        vb = _sb_cache_heads(lambda r: v_buf[r, :])(0)
        _sb_heads(q_ref, [(kb, vb, None)], *state, init=False)
        return j - 1, _sb_alive(later_scr)

    lax.while_loop(more, body, (npast - SB_RECENT_BLOCKS - 1, _sb_alive(later_scr)))
    for h in range(SB_HEADS):
        o_ref[0, :, _sb_cols(h)] = acc_scr[h].astype(o_ref.dtype)


def _sb_sample(q, kvb_new_pad, k_past, v_past, tri_ext, layer):
    bsz, lq, _ = q.shape
    rows = k_past.shape[1]
    npast = rows // SB_BLOCK_ROWS
    assert rows % SB_BLOCK_ROWS == 0 and npast >= SB_RECENT_BLOCKS and lq <= SB_BLOCK
    first = layer * bsz
    recent_rows = SB_RECENT_BLOCKS * SB_BLOCK_ROWS
    assert rows % recent_rows == 0
    recent = pl.BlockSpec((1, recent_rows, SB_HEAD_DIM), lambda b: (first + b, rows // recent_rows - 1, 0))
    return pl.pallas_call(
        functools.partial(_sb_sample_kernel, npast=npast, first=first),
        out_shape=jax.ShapeDtypeStruct((bsz, lq, SB_WIDTH), BF16),
        grid=(bsz,),
        in_specs=[pl.BlockSpec((1, lq, SB_WIDTH), lambda b: (b, 0, 0)),
                  pl.BlockSpec((1, SB_BLOCK, 2 * SB_WIDTH), lambda b: (b, 0, 0)),
                  recent, recent,
                  pl.BlockSpec(memory_space=pl.ANY),
                  pl.BlockSpec(memory_space=pl.ANY),
                  pl.BlockSpec((SB_BLOCK, 2 * SB_BLOCK), lambda b: (0, 0))],
        out_specs=pl.BlockSpec((1, lq, SB_WIDTH), lambda b: (b, 0, 0)),
        scratch_shapes=[pltpu.VMEM((SB_HEADS, lq, SB_BLOCK), F32),
                        pltpu.VMEM((SB_HEADS, lq, SB_HEAD_DIM), F32),
                        pltpu.VMEM((SB_BLOCK_ROWS, SB_HEAD_DIM), F32),
                        pltpu.VMEM((SB_BLOCK_ROWS, SB_HEAD_DIM), F32),
                        pltpu.SemaphoreType.DMA((2,))],
        compiler_params=_cparams(("parallel",)),
        name="sb_sample",
    )(q, kvb_new_pad, k_past, v_past, k_past, v_past, tri_ext)


def _merge_kernel(y_ref, z_ref, gate_ref, attn_ref, x_ref, nw_ref, w1_ref, w2_ref, w3_ref, gpost_ref, o_ref):
    z = z_ref[...].astype(F32)
    g = y_ref[...] * _silu(z)
    b1 = None
    for k in range(SSD_GROUPS):
        gw = slice(k * GROUP_WIDTH, (k + 1) * GROUP_WIDTH)
        gk = _rms(g[:, gw], nw_ref[:, gw])
        part = _dot(gk.astype(BF16), w1_ref[gw, :])
        b1 = part if b1 is None else b1 + part
    b2 = _dot(attn_ref[...], w2_ref[...])
    gl = gate_ref[...].astype(F32)
    m = _sigmoid(gl[:, :D_MODEL]) * b1 + _sigmoid(gl[:, D_MODEL:]) * b2
    mixed = _dot(m.astype(BF16), w3_ref[...])
    o_ref[...] = x_ref[...] + _rms(mixed, gpost_ref[...])


def _merge(y, zg, attn, x, ssd_norm, w_br_ssd, w_br_sb, w_out, norm_post):
    m = x.shape[0]
    tm = min(512, m)
    row = lambda i: (i, 0)
    full = lambda i: (0, 0)
    return pl.pallas_call(
        _merge_kernel,
        out_shape=jax.ShapeDtypeStruct((m, D_MODEL), F32),
        grid=(m // tm,),
        in_specs=[pl.BlockSpec((tm, D_SSM), row),
                  pl.BlockSpec((tm, D_SSM), row),
                  pl.BlockSpec((tm, 2 * D_MODEL), lambda i: (i, 1)),
                  pl.BlockSpec((tm, SB_WIDTH), row),
                  pl.BlockSpec((tm, D_MODEL), row),
                  pl.BlockSpec((1, D_SSM), full),
                  pl.BlockSpec((D_SSM, D_MODEL), full),
                  pl.BlockSpec((SB_WIDTH, D_MODEL), full),
                  pl.BlockSpec((D_MODEL, D_MODEL), full),
                  pl.BlockSpec((1, D_MODEL), full)],
        out_specs=pl.BlockSpec((tm, D_MODEL), row),
        compiler_params=_cparams(("parallel",)),
        name="merge",
    )(y, zg, zg, attn, x, ssd_norm, w_br_ssd, w_br_sb, w_out, norm_post)


def _ffn_kernel(x_ref, st_ref, gpre_ref, wg_ref, wu_ref, cw_ref, cb_ref, wd_ref, gpost_ref,
                p_ref, wple_ref, wpg_ref, gple_ref, o_ref, nst_ref, h_scr, acc_scr, carry_scr):
    l = pl.program_id(1)
    j = pl.program_id(2)
    nb, tl, _ = x_ref.shape

    @pl.when(j == 0)
    def _():
        h_scr[...] = _rms(x_ref[...].reshape(nb * tl, D_MODEL), gpre_ref[...]).astype(BF16)
        acc_scr[...] = jnp.zeros_like(acc_scr)

    @pl.when(l == 0)
    def _():
        carry_scr[j] = st_ref[...]

    h = h_scr[...]
    gate = _dot(h, wg_ref[...])
    up = _dot(h, wu_ref[...])
    conv = _causal_conv(gate, carry_scr[j], cw_ref[...], cb_ref[...])
    tail = gate.reshape(nb, tl, gate.shape[1])[:, tl - SUBLANES:, :]
    carry_scr[j] = tail

    @pl.when(l == pl.num_programs(1) - 1)
    def _():
        nst_ref[...] = tail

    half = 0.5 * conv
    gelu = half + half * jnp.tanh(conv * (GELU_C + (GELU_C * GELU_A) * (conv * conv)))
    acc_scr[...] += _dot((gelu * up).astype(BF16), wd_ref[...])

    @pl.when(j == pl.num_programs(2) - 1)
    def _():
        x1 = x_ref[...].reshape(nb * tl, D_MODEL) + _rms(acc_scr[...], gpost_ref[...])
        emb = _dot(p_ref[...].reshape(nb * tl, PLE_DIM).astype(BF16), wple_ref[...])
        ple = emb * _sigmoid(_dot(x1.astype(BF16), wpg_ref[...]))
        o_ref[...] = (x1 + _rms(ple, gple_ref[...])).reshape(nb, tl, D_MODEL)


def _ffn(x, state8, norm_pre, w_up, conv_w, conv_b, w_down, norm_post, p, w_ple, w_ple_gate, norm_ple,
         *, nb, tl, fc):
    bsz, length, _ = x.shape
    nf = D_FF // fc
    nl = length // tl
    full = lambda b, l, j: (0, 0)
    nst_map = lambda b, l, j: (b, 0, jnp.where(l == nl - 1, j, 0))
    return pl.pallas_call(
        _ffn_kernel,
        out_shape=(jax.ShapeDtypeStruct((bsz, length, D_MODEL), F32),
                   jax.ShapeDtypeStruct((bsz, SUBLANES, D_FF), F32)),
        grid=(bsz // nb, nl, nf),
        in_specs=[pl.BlockSpec((nb, tl, D_MODEL), lambda b, l, j: (b, l, 0)),
                  pl.BlockSpec((nb, SUBLANES, fc), lambda b, l, j: (b, 0, j)),
                  pl.BlockSpec((1, D_MODEL), full),
                  pl.BlockSpec((D_MODEL, fc), lambda b, l, j: (0, j)),
                  pl.BlockSpec((D_MODEL, fc), lambda b, l, j: (0, nf + j)),
                  pl.BlockSpec((FFN_CONV, fc), lambda b, l, j: (0, j)),
                  pl.BlockSpec((1, fc), lambda b, l, j: (0, j)),
                  pl.BlockSpec((fc, D_MODEL), lambda b, l, j: (j, 0)),
                  pl.BlockSpec((1, D_MODEL), full),
                  pl.BlockSpec((nb, tl, PLE_DIM), lambda b, l, j: (b, l, 0)),
                  pl.BlockSpec((PLE_DIM, D_MODEL), full),
                  pl.BlockSpec((D_MODEL, D_MODEL), full),
                  pl.BlockSpec((1, D_MODEL), full)],
        out_specs=(pl.BlockSpec((nb, tl, D_MODEL), lambda b, l, j: (b, l, 0)),
                   pl.BlockSpec((nb, SUBLANES, fc), nst_map)),
        scratch_shapes=[pltpu.VMEM((nb * tl, D_MODEL), BF16),
                        pltpu.VMEM((nb * tl, D_MODEL), F32),
                        pltpu.VMEM((nf, nb, SUBLANES, fc), F32)],
        compiler_params=_cparams(("parallel", "arbitrary", "arbitrary")),
        name="ffn",
    )(x, state8, norm_pre, w_up, w_up, conv_w, conv_b, w_down, norm_post, p, w_ple, w_ple_gate, norm_ple)


def _pad_rows_to8(state):
    return jnp.pad(state, ((0, 0), (SUBLANES - state.shape[1], 0), (0, 0)))


def _pad_lanes(v):
    return jnp.pad(v, (0, LANES - v.shape[0]))[None, :]


def _constants(cl):
    head = jnp.arange(LANES)[:, None]
    chan = jnp.arange(D_SSM)[None, :] // SSD_HEAD_DIM
    expand = (head == chan).astype(BF16)
    idx = jnp.arange(cl)
    tri_cum = (idx[None, :] <= idx[:, None]).astype(BF16)
    kidx = jnp.arange(SB_BLOCK)
    tri_in = (kidx[:, None] > kidx[None, :]).astype(BF16)
    tri_ext = jnp.concatenate([tri_in, jnp.ones((SB_BLOCK, SB_BLOCK), BF16)], axis=1)
    return expand, tri_cum, tri_ext


def _layer(x, p, conv_st, ssm_st, k_cache, v_cache, ffn_st, lw, layer, kv_rows, h_all):
    bsz, length, _ = x.shape
    m = bsz * length
    x2 = x.reshape(m, D_MODEL)
    cl = CHUNK if length >= CHUNK else length
    expand, tri_cum, tri_ext = _constants(cl)

    w_in = lw['w_in']
    o_z, o_xbc, o_dt, o_q, o_g = 0, D_SSM, D_SSM + SSD_CONV_DIM, D_SSM + SSD_CONV_DIM + SSD_HEADS, \
        D_SSM + SSD_CONV_DIM + SSD_HEADS + 3 * SB_WIDTH
    w_main = jnp.concatenate([w_in[:, o_z:o_xbc], w_in[:, o_g:], w_in[:, o_xbc:o_dt], w_in[:, o_q:o_g]],
                             axis=1).astype(BF16)
    w_dt = jnp.pad(w_in[:, o_dt:o_q], ((0, 0), (0, LANES - SSD_HEADS))).astype(BF16)
    g_pre = lw['norm_pre_mix'][None, :]
    zg, xbc, q, kvb, dt_raw, kv_rows = _in_proj(x2, g_pre, w_main, w_dt, layer, kv_rows)

    xbc3 = xbc.reshape(bsz, length, SSD_CONV_DIM)
    tl = min(256, length)
    y, h_all = _ssd(xbc3, dt_raw.reshape(bsz, length, LANES), _pad_rows_to8(conv_st),
                    lw['ssd_conv_w'], lw['ssd_conv_b'][None, :], _pad_lanes(lw['ssd_dt_bias']),
                    _pad_lanes(lw['ssd_a_log']), jnp.repeat(lw['ssd_d'], SSD_HEAD_DIM)[None, :],
                    ssm_st.reshape(bsz, D_SSM, SSD_STATE), expand, tri_cum, h_all, layer, tl=tl, cl=cl)
    tail = 2 * SUBLANES
    x_tail = x[:, length - tail:, :].reshape(bsz * tail, D_MODEL)
    xbc_tail = _norm_matmul(x_tail, g_pre, w_in[:, o_xbc:o_dt].astype(BF16))
    new_conv = xbc_tail.reshape(bsz, tail, SSD_CONV_DIM)[:, tail - (SSD_CONV - 1):]

    q3 = q.reshape(bsz, length, SB_WIDTH)
    kvb3 = kvb.reshape(bsz, length, 2 * SB_WIDTH)
    if k_cache is None:
        attn = _sb_prompt(q3, kvb3, tri_ext)
    else:
        depth, _, past = k_cache.shape[:3]
        kvb_pad = jnp.pad(kvb3, ((0, 0), (0, SB_BLOCK - length), (0, 0)))
        attn = _sb_sample(q3, kvb_pad, k_cache.reshape(depth * bsz, past * SB_HEADS, SB_HEAD_DIM),
                          v_cache.reshape(depth * bsz, past * SB_HEADS, SB_HEAD_DIM), tri_ext, layer)

    x2 = _merge(y.reshape(m, D_SSM), zg, attn.reshape(m, SB_WIDTH), x2, lw['ssd_norm'][None, :],
                lw['w_br_ssd'].astype(BF16), lw['w_br_sb'].astype(BF16), lw['w_out'].astype(BF16),
                lw['norm_post_mix'][None, :])

    x3, ffn8 = _ffn(x2.reshape(bsz, length, D_MODEL), _pad_rows_to8(ffn_st), lw['norm_pre_ffn'][None, :],
                    lw['w_up'].astype(BF16), lw['ffn_conv_w'], lw['ffn_conv_b'][None, :],
                    lw['w_down'].astype(BF16), lw['norm_post_ffn'][None, :],
                    p, lw['w_ple'].astype(BF16), lw['w_ple_gate'].astype(BF16), lw['norm_ple'][None, :],
                    nb=max(1, min(bsz, FFN_ROWS // length)), tl=min(FFN_ROWS, length), fc=512)
    new_ffn = ffn8[:, SUBLANES - (FFN_CONV - 1):]
    return x3, (new_conv, new_ffn), kv_rows, h_all


_LAYER_WEIGHTS = ('norm_pre_mix', 'w_in', 'ssd_conv_w', 'ssd_conv_b', 'ssd_dt_bias', 'ssd_a_log', 'ssd_d',
                  'ssd_norm', 'w_br_ssd', 'w_br_sb', 'w_out', 'norm_post_mix', 'norm_pre_ffn', 'w_up',
                  'ffn_conv_w', 'ffn_conv_b', 'w_down', 'norm_post_ffn', 'w_ple', 'w_ple_gate', 'norm_ple')


def _run(x, p, conv0, ssm0, k0, v0, ffn0, weights):
    bsz, length, _ = x.shape
    per_layer = []
    kv_rows = tuple(jnp.zeros((DEPTH * bsz * length * SB_HEADS, SB_HEAD_DIM), F32) for _ in range(2))
    h_all = jnp.zeros((DEPTH * bsz, D_SSM, SSD_STATE), F32)
    for i in range(DEPTH):
        lw = {name: weights[name][i] for name in _LAYER_WEIGHTS}
        x, st, kv_rows, h_all = _layer(x, p[i], conv0[i], ssm0[i], k0, v0, ffn0[i], lw, i, kv_rows, h_all)
        per_layer.append(st)
    new_conv, new_ffn = (jnp.stack([st[j] for st in per_layer]) for j in range(2))
    new_ssm = h_all.reshape(DEPTH, bsz, SSD_HEADS, SSD_HEAD_DIM, SSD_STATE)
    new_k, new_v = (rows.reshape(DEPTH, bsz, length, SB_HEADS, SB_HEAD_DIM) for rows in kv_rows)
    return x, [new_conv, new_ssm, new_k, new_v, new_ffn]


def kernel(x_prompt, x_sample, state_ssd_conv, state_ssd, cache_sb_k, cache_sb_v, state_ffn_conv, p_prompt, p_sample, norm_pre_mix, w_in, ssd_conv_w, ssd_conv_b, ssd_dt_bias, ssd_a_log, ssd_d, ssd_norm, w_br_ssd, w_br_sb, w_out, norm_post_mix, norm_pre_ffn, w_up, ffn_conv_w, ffn_conv_b, w_down, norm_post_ffn, w_ple, w_ple_gate, norm_ple):
    weights = dict(norm_pre_mix=norm_pre_mix, w_in=w_in, ssd_conv_w=ssd_conv_w, ssd_conv_b=ssd_conv_b,
                   ssd_dt_bias=ssd_dt_bias, ssd_a_log=ssd_a_log, ssd_d=ssd_d, ssd_norm=ssd_norm,
                   w_br_ssd=w_br_ssd, w_br_sb=w_br_sb, w_out=w_out, norm_post_mix=norm_post_mix,
                   norm_pre_ffn=norm_pre_ffn, w_up=w_up, ffn_conv_w=ffn_conv_w, ffn_conv_b=ffn_conv_b,
                   w_down=w_down, norm_post_ffn=norm_post_ffn, w_ple=w_ple, w_ple_gate=w_ple_gate,
                   norm_ple=norm_ple)
    bp = x_prompt.shape[0]
    zero_conv = jnp.zeros((DEPTH, bp, SSD_CONV - 1, SSD_CONV_DIM), F32)
    zero_ssm = jnp.zeros((DEPTH, bp, SSD_HEADS, SSD_HEAD_DIM, SSD_STATE), F32)
    zero_ffn = jnp.zeros((DEPTH, bp, FFN_CONV - 1, D_FF), F32)
    y_prompt, ps = _run(x_prompt, p_prompt, zero_conv, zero_ssm, None, None, zero_ffn, weights)
    y_sample, ss = _run(x_sample, p_sample, state_ssd_conv, state_ssd, cache_sb_k, cache_sb_v,
                        state_ffn_conv, weights)
    return (y_prompt, y_sample, ps[0], ps[1], ps[2], ps[3], ps[4], ss[0], ss[1], ss[2], ss[3], ss[4])
```
